```python
import jax, jax.numpy as jnp
from jax import lax
import numpy as np

D_MODEL = 1024
BATCH = 16
SEQ = 2048
DEPTH = 4
DEC_BATCH = 16
DEC_SEQ = 32
PAST_LEN = 2048

CHUNK = 64
N_META = 16
HEAD_DIM = 64
H_A = 8
KV_A = 2
G_A = H_A // KV_A
WINDOW = 128
WIN_CHUNKS = WINDOW // CHUNK
H_B = 4
DK_B = 64
DV_B = 128
GATE_RANK = 16
GATE_TAU = 16.0
H_C = 16
Q_BLOCK = 128
D_FF = 4 * D_MODEL
ROPE_THETA = 10000.0
EPS = 1e-6
NEG = -1e30
SCALE = HEAD_DIM ** -0.5
N_EVEN = (DEPTH + 1) // 2
N_ODD = DEPTH // 2
AB_SIZES = (H_A * HEAD_DIM, KV_A * HEAD_DIM, KV_A * HEAD_DIM, H_B * DK_B, H_B * DK_B, H_B * DV_B, H_B * DV_B, GATE_RANK)
P_AB = sum(AB_SIZES)
MIX_AB = H_A * HEAD_DIM + H_B * DV_B
C_SIZES = (H_C * HEAD_DIM, H_C * HEAD_DIM, H_C * HEAD_DIM, H_C)
P_C = sum(C_SIZES)
MIX_C = H_C * HEAD_DIM

kernel_name = 'hybrid_stream_swa_gla_fox_step'


def split_cols(z, sizes):
    idx = [int(i) for i in np.cumsum(sizes)[:-1]]
    return jnp.split(z, idx, axis=-1)


def rmsnorm(x, g):
    xf = x.astype(jnp.float32)
    y = xf * lax.rsqrt(jnp.mean(xf * xf, axis=-1, keepdims=True) + EPS)
    return (y * g.astype(jnp.float32)).astype(x.dtype)


def rope(x, pos):
    half = HEAD_DIM // 2
    inv = ROPE_THETA ** (-jnp.arange(half, dtype=jnp.float32) / half)
    ang = pos.astype(jnp.float32)[:, None] * inv[None, :]
    cos = jnp.cos(ang)[None, :, None, :]
    sin = jnp.sin(ang)[None, :, None, :]
    xf = x.astype(jnp.float32)
    x1, x2 = xf[..., :half], xf[..., half:]
    return jnp.concatenate([x1 * cos - x2 * sin, x2 * cos + x1 * sin], axis=-1).astype(x.dtype)


def mlp(h, w_up, w_down):
    u = jax.nn.relu(h @ w_up)
    return (u * u) @ w_down


def ab_project(h, w_in, qn, kn, w_gate, b_gate, pos):
    B_, L = h.shape[:2]
    qa, ka, va, qb, kb, vb, rb, glr = split_cols(h @ w_in, AB_SIZES)
    qa = rope(rmsnorm(qa.reshape(B_, L, H_A, HEAD_DIM), qn), pos)
    ka = rope(rmsnorm(ka.reshape(B_, L, KV_A, HEAD_DIM), kn), pos)
    va = va.reshape(B_, L, KV_A, HEAD_DIM)
    qb = qb.reshape(B_, L, H_B, DK_B) * (DK_B ** -0.5)
    kb = kb.reshape(B_, L, H_B, DK_B)
    vb = vb.reshape(B_, L, H_B, DV_B)
    gb = jax.nn.log_sigmoid((glr @ w_gate + b_gate).astype(jnp.float32)) / GATE_TAU
    gb = gb.reshape(B_, L, H_B, DK_B)
    return qa, ka, va, qb, kb, vb, rb, gb


def sink_attend(s, sink, v, eq):
    m = jnp.maximum(jnp.max(s, axis=-1, keepdims=True), sink)
    p = jnp.exp(s - m)
    den = jnp.sum(p, axis=-1, keepdims=True) + jnp.exp(sink - m)
    return jnp.einsum(eq, p / den, v.astype(jnp.float32))


def swa_prompt(q, k, v, sink):
    B_, L = q.shape[:2]
    lead = CHUNK - N_META
    nb = (L + lead) // CHUNK
    front = lead + WIN_CHUNKS * CHUNK
    qb = jnp.pad(q, ((0, 0), (lead, 0), (0, 0), (0, 0))).reshape(B_, nb, CHUNK, KV_A, G_A, HEAD_DIM)
    kp = jnp.pad(k, ((0, 0), (front, 0), (0, 0), (0, 0))).reshape(B_, nb + WIN_CHUNKS, CHUNK, KV_A, HEAD_DIM)
    vp = jnp.pad(v, ((0, 0), (front, 0), (0, 0), (0, 0))).reshape(B_, nb + WIN_CHUNKS, CHUNK, KV_A, HEAD_DIM)
    valid = (jnp.arange((nb + WIN_CHUNKS) * CHUNK) >= front).reshape(nb + WIN_CHUNKS, CHUNK)
    kband = jnp.concatenate([kp[:, i:i + nb] for i in range(WIN_CHUNKS + 1)], axis=2)
    vband = jnp.concatenate([vp[:, i:i + nb] for i in range(WIN_CHUNKS + 1)], axis=2)
    vmask = jnp.concatenate([valid[i:i + nb] for i in range(WIN_CHUNKS + 1)], axis=1)
    s = jnp.einsum('bnqhgd,bnshd->bnhgqs', qb, kband, preferred_element_type=jnp.float32) * SCALE
    s = jnp.where(vmask[None, :, None, None, None, :], s, NEG)
    sk = sink.astype(jnp.float32).reshape(1, 1, KV_A, G_A, 1, 1)
    o = sink_attend(s, sk, vband, 'bnhgqs,bnshd->bnqhgd')
    return o.reshape(B_, nb * CHUNK, H_A * HEAD_DIM)[:, lead:].astype(q.dtype)


def swa_sample(q, k_new, v_new, ck, cv, sink):
    B_, T = q.shape[:2]
    kk = jnp.concatenate([ck.astype(k_new.dtype), k_new], axis=1)
    vv = jnp.concatenate([cv.astype(v_new.dtype), v_new], axis=1)
    qg = q.reshape(B_, T, KV_A, G_A, HEAD_DIM)
    s = jnp.einsum('bqhgd,bshd->bhgqs', qg, kk, preferred_element_type=jnp.float32) * SCALE
    sk = sink.astype(jnp.float32).reshape(1, KV_A, G_A, 1, 1)
    o = sink_attend(s, sk, vv, 'bhgqs,bshd->bqhgd')
    return o.reshape(B_, T, H_A * HEAD_DIM).astype(q.dtype), kk[:, -WINDOW:], vv[:, -WINDOW:]


def gla_chunk(S, q, k, v, g):
    C = q.shape[2]
    b = jnp.cumsum(g, axis=2)
    causal = jnp.tril(jnp.ones((C, C), dtype=bool))
    diff = b[:, :, :, None, :] - b[:, :, None, :, :]
    decay = jnp.exp(jnp.where(causal[:, :, None], diff, -jnp.inf))
    att = jnp.einsum('bhtd,bhsd,bhtsd->bhts', q, k, decay)
    o = jnp.einsum('bhtd,bhdv->bhtv', q * jnp.exp(b), S) + jnp.einsum('bhts,bhsv->bhtv', att, v)
    bl = b[:, :, -1:, :]
    S_new = jnp.exp(bl[:, :, 0, :, None]) * S + jnp.einsum('bhsd,bhsv->bhdv', k * jnp.exp(bl - b), v)
    return S_new, o


def gla_prompt(q, k, v, g):
    B_, L = q.shape[:2]
    lead = CHUNK - N_META
    f32 = jnp.float32
    padf = lambda t: jnp.pad(t.astype(f32), ((0, 0), (lead, 0), (0, 0), (0, 0)))
    Lp = L + lead
    nb = Lp // CHUNK
    to_chunks = lambda t: t.reshape(B_, nb, CHUNK, H_B, t.shape[-1]).transpose(1, 0, 3, 2, 4)
    xs = (to_chunks(padf(q)), to_chunks(padf(k)), to_chunks(padf(v)), to_chunks(padf(g)))
    S0 = jnp.zeros((B_, H_B, DK_B, DV_B), f32)
    S, o = lax.scan(lambda S, c: gla_chunk(S, *c), S0, xs)
    o = o.transpose(1, 0, 3, 2, 4).reshape(B_, Lp, H_B, DV_B)[:, lead:]
    return o.astype(q.dtype), S


def gla_sample(q, k, v, g, S):
    f32 = jnp.float32
    tr = lambda t: t.astype(f32).transpose(0, 2, 1, 3)
    S_new, o = gla_chunk(S.astype(f32), tr(q), tr(k), tr(v), tr(g))
    return o.transpose(0, 2, 1, 3).astype(q.dtype), S_new


def ab_output(oa, ob, rb, onorm, w_out):
    B_, L = oa.shape[:2]
    ob = rmsnorm(ob, onorm).reshape(B_, L, H_B * DV_B) * jax.nn.silu(rb)
    return jnp.concatenate([oa, ob], axis=-1) @ w_out


def c_project(h, w_in, b_f, qn, kn):
    B_, L = h.shape[:2]
    q, k, v, fl = split_cols(h @ w_in, C_SIZES)
    q = rmsnorm(q.reshape(B_, L, H_C, HEAD_DIM), qn)
    k = rmsnorm(k.reshape(B_, L, H_C, HEAD_DIM), kn)
    v = v.reshape(B_, L, H_C, HEAD_DIM)
    logf = jax.nn.log_sigmoid((fl + b_f).astype(jnp.float32))
    return q, k, v, logf


def fox_prompt(q, k, v, logf):
    B_, L = q.shape[:2]
    Lp = -(-L // Q_BLOCK) * Q_BLOCK
    padL = lambda t: jnp.pad(t, ((0, 0), (0, Lp - L)) + ((0, 0),) * (t.ndim - 2))
    qp, kp, vp = padL(q), padL(k), padL(v).astype(jnp.float32)
    ct = padL(jnp.cumsum(logf, axis=1)).transpose(0, 2, 1)
    kpos = jnp.arange(Lp)

    def block(i):
        start = i * Q_BLOCK
        qb = lax.dynamic_slice_in_dim(qp, start, Q_BLOCK, axis=1)
        cq = lax.dynamic_slice_in_dim(ct, start, Q_BLOCK, axis=2)
        s = jnp.einsum('bqhd,bkhd->bhqk', qb, kp, preferred_element_type=jnp.float32) * SCALE
        s = s + cq[..., None] - ct[:, :, None, :]
        qpos = start + jnp.arange(Q_BLOCK)
        s = jnp.where(kpos[None, :] <= qpos[:, None], s, NEG)
        p = jax.nn.softmax(s, axis=-1)
        return jnp.einsum('bhqk,bkhd->bqhd', p, vp)

    o = lax.map(block, jnp.arange(Lp // Q_BLOCK))
    o = o.transpose(1, 0, 2, 3, 4).reshape(B_, Lp, MIX_C)[:, :L]
    return o.astype(q.dtype)


def fox_sample(q, k_new, v_new, logf_new, ck, cv, clogf):
    B_, T = q.shape[:2]
    P = ck.shape[1]
    kk = jnp.concatenate([ck.astype(k_new.dtype), k_new], axis=1)
    vv = jnp.concatenate([cv.astype(v_new.dtype), v_new], axis=1).astype(jnp.float32)
    c = jnp.cumsum(jnp.concatenate([clogf.astype(jnp.float32), logf_new], axis=1), axis=1).transpose(0, 2, 1)
    s = jnp.einsum('bqhd,bkhd->bhqk', q, kk, preferred_element_type=jnp.float32) * SCALE
    s = s + c[:, :, P:, None] - c[:, :, None, :]
    mask = jnp.arange(P + T)[None, :] <= (P + jnp.arange(T))[:, None]
    p = jax.nn.softmax(jnp.where(mask, s, NEG), axis=-1)
    o = jnp.einsum('bhqk,bkhd->bqhd', p, vv)
    return o.reshape(B_, T, MIX_C).astype(q.dtype)


def setup_inputs(seed: int = 0) -> dict:
    key = jax.random.key(seed)
    ks = jax.random.split(key, 32)
    f32 = jnp.float32
    nrm = lambda k, shape, s=1.0: jax.random.normal(k, shape, f32) * s
    return {
        'x_prompt': nrm(ks[0], (BATCH, SEQ, D_MODEL)),
        'x_sample': nrm(ks[1], (DEC_BATCH, DEC_SEQ, D_MODEL)),
        'cache_a_k': nrm(ks[2], (N_EVEN, DEC_BATCH, WINDOW, KV_A, HEAD_DIM)),
        'cache_a_v': nrm(ks[3], (N_EVEN, DEC_BATCH, WINDOW, KV_A, HEAD_DIM)),
        'state_b': nrm(ks[4], (N_EVEN, DEC_BATCH, H_B, DK_B, DV_B), 0.5),
        'cache_c_k': nrm(ks[5], (N_ODD, DEC_BATCH, PAST_LEN, H_C, HEAD_DIM)),
        'cache_c_v': nrm(ks[6], (N_ODD, DEC_BATCH, PAST_LEN, H_C, HEAD_DIM)),
        'cache_c_logf': jax.nn.log_sigmoid(2.0 + nrm(ks[7], (N_ODD, DEC_BATCH, PAST_LEN, H_C))),
        'meta_tokens': nrm(ks[8], (N_META, D_MODEL)),
        'norm_mix': 1.0 + nrm(ks[9], (DEPTH, D_MODEL), 0.1),
        'norm_mlp': 1.0 + nrm(ks[10], (DEPTH, D_MODEL), 0.1),
        'w_in_ab': nrm(ks[11], (N_EVEN, D_MODEL, P_AB), D_MODEL ** -0.5),
        'qnorm_a': 1.0 + nrm(ks[12], (N_EVEN, HEAD_DIM), 0.1),
        'knorm_a': 1.0 + nrm(ks[13], (N_EVEN, HEAD_DIM), 0.1),
        'sink_a': nrm(ks[14], (N_EVEN, H_A), 0.5),
        'w_gate_b': nrm(ks[15], (N_EVEN, GATE_RANK, H_B * DK_B), GATE_RANK ** -0.5),
        'b_gate_b': nrm(ks[16], (N_EVEN, H_B * DK_B), 0.1),
        'onorm_b': 1.0 + nrm(ks[17], (N_EVEN, DV_B), 0.1),
        'w_out_ab': nrm(ks[18], (N_EVEN, MIX_AB, D_MODEL), MIX_AB ** -0.5),
        'w_in_c': nrm(ks[19], (N_ODD, D_MODEL, P_C), D_MODEL ** -0.5),
        'b_f_c': 2.0 + nrm(ks[20], (N_ODD, H_C), 0.1),
        'qnorm_c': 1.0 + nrm(ks[21], (N_ODD, HEAD_DIM), 0.1),
        'knorm_c': 1.0 + nrm(ks[22], (N_ODD, HEAD_DIM), 0.1),
        'w_out_c': nrm(ks[23], (N_ODD, MIX_C, D_MODEL), MIX_C ** -0.5),
        'w_up': nrm(ks[24], (DEPTH, D_MODEL, D_FF), D_MODEL ** -0.5),
        'w_down': nrm(ks[25], (DEPTH, D_FF, D_MODEL), D_FF ** -0.5),
    }


def reference(x_prompt, x_sample, cache_a_k, cache_a_v, state_b, cache_c_k, cache_c_v, cache_c_logf,
              meta_tokens, norm_mix, norm_mlp, w_in_ab, qnorm_a, knorm_a, sink_a, w_gate_b, b_gate_b,
              onorm_b, w_out_ab, w_in_c, b_f_c, qnorm_c, knorm_c, w_out_c, w_up, w_down):
    Bp = x_prompt.shape[0]
    meta = jnp.broadcast_to(meta_tokens.astype(x_prompt.dtype)[None], (Bp, N_META, D_MODEL))
    xp = jnp.concatenate([meta, x_prompt], axis=1)
    xs = x_sample
    T = xs.shape[1]
    pos_p = jnp.arange(xp.shape[1])
    pos_s = N_META + PAST_LEN + jnp.arange(T)
    akp, avp, bp, ckp, cvp, cfp = [], [], [], [], [], []
    aks, avs, bs, cks, cvs, cfs = [], [], [], [], [], []
    for l in range(DEPTH):
        i = l // 2
        hp = rmsnorm(xp, norm_mix[l])
        hs = rmsnorm(xs, norm_mix[l])
        if l % 2 == 0:
            qa, ka, va, qb, kb, vb, rb, gb = ab_project(hp, w_in_ab[i], qnorm_a[i], knorm_a[i], w_gate_b[i], b_gate_b[i], pos_p)
            oa = swa_prompt(qa, ka, va, sink_a[i])
            ob, Sp = gla_prompt(qb, kb, vb, gb)
            xp = xp + ab_output(oa, ob, rb, onorm_b[i], w_out_ab[i])
            akp.append(ka[:, -WINDOW:])
            avp.append(va[:, -WINDOW:])
            bp.append(Sp.astype(xp.dtype))
            qa, ka, va, qb, kb, vb, rb, gb = ab_project(hs, w_in_ab[i], qnorm_a[i], knorm_a[i], w_gate_b[i], b_gate_b[i], pos_s)
            oa, nk, nv = swa_sample(qa, ka, va, cache_a_k[i], cache_a_v[i], sink_a[i])
            ob, Ss = gla_sample(qb, kb, vb, gb, state_b[i])
            xs = xs + ab_output(oa, ob, rb, onorm_b[i], w_out_ab[i])
            aks.append(nk)
            avs.append(nv)
            bs.append(Ss.astype(xs.dtype))
        else:
            q, k, v, lf = c_project(hp, w_in_c[i], b_f_c[i], qnorm_c[i], knorm_c[i])
            xp = xp + fox_prompt(q, k, v, lf) @ w_out_c[i]
            ckp.append(k)
            cvp.append(v)
            cfp.append(lf.astype(xp.dtype))
            q, k, v, lf = c_project(hs, w_in_c[i], b_f_c[i], qnorm_c[i], knorm_c[i])
            xs = xs + fox_sample(q, k, v, lf, cache_c_k[i], cache_c_v[i], cache_c_logf[i]) @ w_out_c[i]
            cks.append(k)
            cvs.append(v)
            cfs.append(lf.astype(xs.dtype))
        xp = xp + mlp(rmsnorm(xp, norm_mlp[l]), w_up[l], w_down[l])
        xs = xs + mlp(rmsnorm(xs, norm_mlp[l]), w_up[l], w_down[l])
    y_prompt = xp[:, N_META:]
    return (y_prompt, xs,
            jnp.stack(akp), jnp.stack(avp), jnp.stack(bp), jnp.stack(ckp), jnp.stack(cvp), jnp.stack(cfp),
            jnp.stack(aks), jnp.stack(avs), jnp.stack(bs), jnp.stack(cks), jnp.stack(cvs), jnp.stack(cfs))
```

```python
import functools

import numpy as np
import jax
import jax.numpy as jnp
from jax import lax
from jax.experimental import pallas as pl
from jax.experimental.pallas import tpu as pltpu

F32 = jnp.float32
BF16 = jnp.bfloat16

D_MODEL = 1024
CHUNK = 64
N_META = 16
HEAD_DIM = 64
H_A = 8
KV_A = 2
G_A = H_A // KV_A
WINDOW = 128
H_B = 4
DK_B = 64
DV_B = 128
GATE_RANK = 16
GATE_TAU = 16.0
H_C = 16
D_FF = 4 * D_MODEL
ROPE_THETA = 10000.0
EPS = 1e-6
NEG = -1e30
SCALE = HEAD_DIM ** -0.5
QA_W = H_A * HEAD_DIM
KA_W = KV_A * HEAD_DIM
QB_W = H_B * DK_B
VB_W = H_B * DV_B
AB_MAIN = QA_W + 2 * KA_W + 2 * QB_W + 2 * VB_W
C_W = H_C * HEAD_DIM

LANES = 128
MXU_DIM = 256
VMEM_LIMIT = 56 * 1024 * 1024
SUB_BLOCK = 16


def _cparams(n_axes):
    return pltpu.CompilerParams(
        dimension_semantics=("arbitrary",) * n_axes, vmem_limit_bytes=VMEM_LIMIT)


def _const_spec(shape):
    nd = len(shape)
    return pl.BlockSpec(shape, lambda *_: (0,) * nd, pipeline_mode=pl.Buffered(1))


def _rmsnorm_rows(x, g):
    ms = jnp.mean(x * x, axis=-1, keepdims=True)
    return x * lax.rsqrt(ms + EPS) * g


def _log_sigmoid(x):
    return jnp.minimum(x, 0.0) - jnp.log1p(jnp.exp(-jnp.abs(x)))


def _head_mean_square(x, bd_ref):
    xx = (x * x).astype(BF16)
    w = x.shape[1]
    parts = []
    for s in range(0, w, MXU_DIM):
        e = min(s + MXU_DIM, w)
        parts.append(jnp.dot(xx[:, s:e], bd_ref[0:e - s, 0:e - s], preferred_element_type=F32))
    return parts[0] if len(parts) == 1 else jnp.concatenate(parts, axis=1)


def _rotate_half(x):
    w = x.shape[1]
    lane = lax.broadcasted_iota(jnp.int32, x.shape, 1)
    first = (lane % HEAD_DIM) < (HEAD_DIM // 2)
    return jnp.where(first, pltpu.roll(x, w - HEAD_DIM // 2, 1), pltpu.roll(x, HEAD_DIM // 2, 1))


def _proj_ab_body(x_ref, g_ref, w_ref, wr_ref, bd_ref, qn_ref, kn_ref, cos_ref, sin_ref,
                  wgate_ref, bgate_ref, qa_ref, kva_ref, qkg_ref, vb_ref, rb_ref):
    h = _rmsnorm_rows(x_ref[...], g_ref[...]).astype(BF16)

    z = jnp.dot(h, w_ref[:, 0:QA_W + 2 * KA_W], preferred_element_type=F32)
    cos = cos_ref[...]
    sin = sin_ref[...]

    qa = z[:, 0:QA_W]
    qa = qa * lax.rsqrt(_head_mean_square(qa, bd_ref) + EPS) * qn_ref[...]
    cos_q = jnp.concatenate([cos] * (QA_W // LANES), axis=1)
    sin_q = jnp.concatenate([sin] * (QA_W // LANES), axis=1)
    qa_ref[...] = (qa * cos_q + _rotate_half(qa) * sin_q).astype(qa_ref.dtype)

    ka = z[:, QA_W:QA_W + KA_W]
    ka = ka * lax.rsqrt(_head_mean_square(ka, bd_ref) + EPS) * kn_ref[...]
    kva_ref[:, 0:KA_W] = ka * cos + _rotate_half(ka) * sin
    kva_ref[:, KA_W:2 * KA_W] = z[:, QA_W + KA_W:QA_W + 2 * KA_W]

    c0 = QA_W + 2 * KA_W
    z = jnp.dot(h, w_ref[:, c0:c0 + 2 * QB_W], preferred_element_type=F32)
    qkg_ref[:, 0:QB_W] = z[:, 0:QB_W] * (DK_B ** -0.5)
    qkg_ref[:, QB_W:2 * QB_W] = z[:, QB_W:2 * QB_W]
    glr = jnp.dot(h, wr_ref[...], preferred_element_type=F32)
    gl = jnp.dot(glr.astype(BF16), wgate_ref[...], preferred_element_type=F32) + bgate_ref[...]
    qkg_ref[:, 2 * QB_W:3 * QB_W] = _log_sigmoid(gl) * (1.0 / GATE_TAU)

    c1 = c0 + 2 * QB_W
    vb_ref[...] = jnp.dot(h, w_ref[:, c1:c1 + VB_W], preferred_element_type=F32)
    rb_ref[...] = jnp.dot(h, w_ref[:, c1 + VB_W:c1 + 2 * VB_W], preferred_element_type=F32)


def _proj_ab(x, g, w_main, w_rank, bd, qn, kn, cos, sin, wgate, bgate, *, tm, n_pos_tiles):
    m = x.shape[0]
    row = lambda w: pl.BlockSpec((tm, w), lambda i: (i, 0))
    pos = pl.BlockSpec((tm, LANES), lambda i: (i % n_pos_tiles, 0))
    return pl.pallas_call(
        _proj_ab_body,
        grid=(m // tm,),
        in_specs=[row(D_MODEL), _const_spec((1, D_MODEL)), _const_spec(w_main.shape),
                  _const_spec(w_rank.shape), _const_spec(bd.shape), _const_spec(qn.shape),
                  _const_spec(kn.shape), pos, pos, _const_spec(wgate.shape), _const_spec(bgate.shape)],
        out_specs=[row(QA_W), row(2 * KA_W), row(3 * QB_W), row(VB_W), row(VB_W)],
        out_shape=[jax.ShapeDtypeStruct((m, QA_W), BF16), jax.ShapeDtypeStruct((m, 2 * KA_W), F32),
                   jax.ShapeDtypeStruct((m, 3 * QB_W), F32), jax.ShapeDtypeStruct((m, VB_W), F32),
                   jax.ShapeDtypeStruct((m, VB_W), F32)],
        compiler_params=_cparams(1),
        name="proj_ab",
    )(x, g, w_main, w_rank, bd, qn, kn, cos, sin, wgate, bgate)


def _proj_c_body(x_ref, g_ref, w_ref, wf_ref, bf_ref, bd_ref, qn_ref, kn_ref,
                 q_ref, k_ref, v_ref, lf_ref):
    h = _rmsnorm_rows(x_ref[...], g_ref[...]).astype(BF16)
    for s in range(0, C_W, MXU_DIM):
        zq = jnp.dot(h, w_ref[:, s:s + MXU_DIM], preferred_element_type=F32)
        zq = zq * lax.rsqrt(_head_mean_square(zq, bd_ref) + EPS) * qn_ref[...]
        q_ref[:, s:s + MXU_DIM] = (zq * SCALE).astype(q_ref.dtype)
        zk = jnp.dot(h, w_ref[:, C_W + s:C_W + s + MXU_DIM], preferred_element_type=F32)
        k_ref[:, s:s + MXU_DIM] = zk * lax.rsqrt(_head_mean_square(zk, bd_ref) + EPS) * kn_ref[...]
    v_ref[...] = jnp.dot(h, w_ref[:, 2 * C_W:3 * C_W], preferred_element_type=F32)
    fl = jnp.dot(h, wf_ref[...], preferred_element_type=F32) + bf_ref[...]
    lf_ref[...] = _log_sigmoid(fl)


def _proj_c(x, g, w_main, w_f, b_f, bd, qn, kn, *, tm):
    m = x.shape[0]
    row = lambda w: pl.BlockSpec((tm, w), lambda i: (i, 0))
    return pl.pallas_call(
        _proj_c_body,
        grid=(m // tm,),
        in_specs=[row(D_MODEL), _const_spec((1, D_MODEL)), _const_spec(w_main.shape),
                  _const_spec(w_f.shape), _const_spec(b_f.shape), _const_spec(bd.shape),
                  _const_spec(qn.shape), _const_spec(kn.shape)],
        out_specs=[row(C_W), row(C_W), row(C_W), row(LANES)],
        out_shape=[jax.ShapeDtypeStruct((m, C_W), BF16), jax.ShapeDtypeStruct((m, C_W), F32),
                   jax.ShapeDtypeStruct((m, C_W), F32), jax.ShapeDtypeStruct((m, LANES), F32)],
        compiler_params=_cparams(1),
        name="proj_c",
    )(x, g, w_main, w_f, b_f, bd, qn, kn)


def _mlp_tail(x1, g2_ref, wup_ref, wdown_ref, out_ref):
    h2 = _rmsnorm_rows(x1, g2_ref[...]).astype(BF16)
    acc = x1
    for c in range(0, D_FF, D_MODEL):
        u = jnp.dot(h2, wup_ref[:, c:c + D_MODEL], preferred_element_type=F32)
        u = jnp.maximum(u, 0.0)
        acc = acc + jnp.dot((u * u).astype(BF16), wdown_ref[c:c + D_MODEL, :], preferred_element_type=F32)
    out_ref[...] = acc


def _post_ab_body(x_ref, oa_ref, ob_ref, rb_ref, on_ref, wout_ref, g2_ref, wup_ref, wdown_ref, out_ref):
    ob = ob_ref[...]
    parts = []
    for hb in range(H_B):
        sl = ob[:, hb * DV_B:(hb + 1) * DV_B]
        parts.append(sl * lax.rsqrt(jnp.mean(sl * sl, axis=-1, keepdims=True) + EPS) * on_ref[...])
    rb = rb_ref[...]
    gate = rb / (1.0 + jnp.exp(-rb))
    obn = (jnp.concatenate(parts, axis=1) * gate).astype(BF16)
    y = jnp.dot(oa_ref[...], wout_ref[0:QA_W, :], preferred_element_type=F32)
    y = y + jnp.dot(obn, wout_ref[QA_W:QA_W + VB_W, :], preferred_element_type=F32)
    _mlp_tail(x_ref[...] + y, g2_ref, wup_ref, wdown_ref, out_ref)


def _post_c_body(x_ref, o_ref, wout_ref, g2_ref, wup_ref, wdown_ref, out_ref):
    y = jnp.dot(o_ref[...], wout_ref[...], preferred_element_type=F32)
    _mlp_tail(x_ref[...] + y, g2_ref, wup_ref, wdown_ref, out_ref)


def _post_ab(x, oa, ob, rb, onorm, wout, g2, wup, wdown, *, tm):
    m = x.shape[0]
    row = lambda w: pl.BlockSpec((tm, w), lambda i: (i, 0))
    return pl.pallas_call(
        _post_ab_body,
        grid=(m // tm,),
        in_specs=[row(D_MODEL), row(QA_W), row(VB_W), row(VB_W), _const_spec(onorm.shape),
                  _const_spec(wout.shape), _const_spec(g2.shape), _const_spec(wup.shape),
                  _const_spec(wdown.shape)],
        out_specs=row(D_MODEL),
        out_shape=jax.ShapeDtypeStruct((m, D_MODEL), F32),
        compiler_params=_cparams(1),
        name="post_ab",
    )(x, oa, ob, rb, onorm, wout, g2, wup, wdown)


def _post_c(x, o, wout, g2, wup, wdown, *, tm):
    m = x.shape[0]
    row = lambda w: pl.BlockSpec((tm, w), lambda i: (i, 0))
    return pl.pallas_call(
        _post_c_body,
        grid=(m // tm,),
        in_specs=[row(D_MODEL), row(C_W), _const_spec(wout.shape), _const_spec(g2.shape),
                  _const_spec(wup.shape), _const_spec(wdown.shape)],
        out_specs=row(D_MODEL),
        out_shape=jax.ShapeDtypeStruct((m, D_MODEL), F32),
        compiler_params=_cparams(1),
        name="post_c",
    )(x, o, wout, g2, wup, wdown)


SWA_FRONT = CHUNK - N_META + (WINDOW // CHUNK) * CHUNK
SWA_BAND = WINDOW + CHUNK


def _dup_heads(x, hk):
    lane = lax.broadcasted_iota(jnp.int32, x.shape, 1)
    own = (lane < HEAD_DIM) if hk == 0 else (lane >= HEAD_DIM)
    return jnp.where(own, x, pltpu.roll(x, HEAD_DIM, 1))


def _swa_block(qs, kband, vband, valid, sinks):
    tq = qs.shape[0]
    lane = lax.broadcasted_iota(jnp.int32, (tq, LANES), 1)
    even = lane < HEAD_DIM
    zero = jnp.zeros((tq, LANES), qs.dtype)
    s0, s1 = qs[:, 0:LANES], qs[:, LANES:2 * LANES]
    qz = jnp.concatenate([jnp.where(even, s0, zero), jnp.where(even, zero, s0),
                          jnp.where(even, s1, zero), jnp.where(even, zero, s1)], axis=0)
    s = lax.dot_general(qz, kband, (((1,), (1,)), ((), ())), preferred_element_type=F32) * SCALE
    if valid is not None:
        s = jnp.where(valid, s, NEG)
    rowh = lax.broadcasted_iota(jnp.int32, (4 * tq, 1), 0) // tq
    sk = jnp.where(rowh == 0, sinks[0], jnp.where(rowh == 1, sinks[1],
                                                   jnp.where(rowh == 2, sinks[2], sinks[3])))
    m = jnp.maximum(jnp.max(s, axis=-1, keepdims=True), sk)
    p = jnp.exp(s - m)
    den = jnp.sum(p, axis=-1, keepdims=True) + jnp.exp(sk - m)
    o = jnp.dot(p.astype(BF16), vband, preferred_element_type=F32) / den
    return (jnp.where(even, o[0:tq], o[tq:2 * tq]),
            jnp.where(even, o[2 * tq:3 * tq], o[3 * tq:4 * tq]))


def _swa_prompt_body(sink_ref, q_ref, kv_ref, o_ref, kd_ref, vd_ref):
    seq = q_ref.shape[0]
    ka = kv_ref[:, 0:KA_W]
    va = kv_ref[:, KA_W:2 * KA_W]
    for hk in range(KV_A):
        kd_ref[hk, 0:SWA_FRONT, :] = jnp.zeros((SWA_FRONT, LANES), BF16)
        vd_ref[hk, 0:SWA_FRONT, :] = jnp.zeros((SWA_FRONT, LANES), BF16)
        kd_ref[hk, SWA_FRONT:SWA_FRONT + seq, :] = _dup_heads(ka, hk).astype(BF16)
        vd_ref[hk, SWA_FRONT:SWA_FRONT + seq, :] = _dup_heads(va, hk).astype(BF16)

    def block(r0, tq, kb0):
        col = lax.broadcasted_iota(jnp.int32, (4 * tq, SWA_BAND), 1)
        valid = (col + kb0) >= SWA_FRONT
        for hk in range(KV_A):
            sinks = [sink_ref[G_A * hk + g] for g in range(G_A)]
            c0 = hk * G_A * HEAD_DIM
            qs = q_ref[pl.ds(r0, tq), c0:c0 + G_A * HEAD_DIM]
            kband = kd_ref[hk, pl.ds(kb0, SWA_BAND), :]
            vband = vd_ref[hk, pl.ds(kb0, SWA_BAND), :]
            o0, o1 = _swa_block(qs, kband, vband, valid, sinks)
            o_ref[pl.ds(r0, tq), c0:c0 + LANES] = o0.astype(o_ref.dtype)
            o_ref[pl.ds(r0, tq), c0 + LANES:c0 + 2 * LANES] = o1.astype(o_ref.dtype)

    block(0, N_META, 0)

    def chunk(c, carry):
        block(pl.multiple_of(N_META + c * CHUNK, 16), CHUNK, pl.multiple_of((c + 1) * CHUNK, CHUNK))
        return carry

    lax.fori_loop(0, (seq - N_META) // CHUNK, chunk, 0)


def _swa_prompt(sink, qa, kva, *, batch):
    m = qa.shape[0]
    seq = m // batch
    return pl.pallas_call(
        _swa_prompt_body,
        grid_spec=pltpu.PrefetchScalarGridSpec(
            num_scalar_prefetch=1,
            grid=(batch,),
            in_specs=[pl.BlockSpec((seq, QA_W), lambda b, s: (b, 0)),
                      pl.BlockSpec((seq, 2 * KA_W), lambda b, s: (b, 0))],
            out_specs=pl.BlockSpec((seq, QA_W), lambda b, s: (b, 0)),
            scratch_shapes=[pltpu.VMEM((KV_A, SWA_FRONT + seq, LANES), BF16),
                            pltpu.VMEM((KV_A, SWA_FRONT + seq, LANES), BF16)]),
        out_shape=jax.ShapeDtypeStruct((m, QA_W), BF16),
        compiler_params=_cparams(1),
        name="swa_prompt",
    )(sink, qa, kva)


def _swa_sample_body(sink_ref, q_ref, kv_ref, ck_ref, cv_ref, o_ref, nk_ref, nv_ref):
    t = q_ref.shape[0]
    kk = jnp.concatenate([ck_ref[0], kv_ref[:, 0:KA_W]], axis=0)
    vv = jnp.concatenate([cv_ref[0], kv_ref[:, KA_W:2 * KA_W]], axis=0)
    nk_ref[0] = kk[t:, :]
    nv_ref[0] = vv[t:, :]
    for hk in range(KV_A):
        sinks = [sink_ref[G_A * hk + g] for g in range(G_A)]
        c0 = hk * G_A * HEAD_DIM
        o0, o1 = _swa_block(q_ref[:, c0:c0 + G_A * HEAD_DIM], _dup_heads(kk, hk).astype(BF16),
                            _dup_heads(vv, hk).astype(BF16), None, sinks)
        o_ref[:, c0:c0 + LANES] = o0.astype(o_ref.dtype)
        o_ref[:, c0 + LANES:c0 + 2 * LANES] = o1.astype(o_ref.dtype)


def _swa_sample(sink, qa, kva, cache_k, cache_v, *, batch):
    m = qa.shape[0]
    t = m // batch
    cache = pl.BlockSpec((1, WINDOW, KA_W), lambda b, s: (b, 0, 0))
    return pl.pallas_call(
        _swa_sample_body,
        grid_spec=pltpu.PrefetchScalarGridSpec(
            num_scalar_prefetch=1,
            grid=(batch,),
            in_specs=[pl.BlockSpec((t, QA_W), lambda b, s: (b, 0)),
                      pl.BlockSpec((t, 2 * KA_W), lambda b, s: (b, 0)), cache, cache],
            out_specs=[pl.BlockSpec((t, QA_W), lambda b, s: (b, 0)), cache, cache]),
        out_shape=[jax.ShapeDtypeStruct((m, QA_W), BF16),
                   jax.ShapeDtypeStruct((batch, WINDOW, KA_W), F32),
                   jax.ShapeDtypeStruct((batch, WINDOW, KA_W), F32)],
        compiler_params=_cparams(1),
        name="swa_sample",
    )(sink, qa, kva, cache_k, cache_v)


def _gla_nodes(t_len):
    nodes = []
    span = t_len
    while span > SUB_BLOCK:
        half = span // 2
        for start in range(0, t_len, span):
            nodes.append((start + half, start + span, start, start + half, start + half - 1))
        span = half
    return nodes


def _gla_chunk(q, k, g, v, st, tri_ref, ones_ref, stmask_ref):
    t_len = q.shape[0]
    nt = (((1,), (1,)), ((), ()))
    row = lax.broadcasted_iota(jnp.int32, (t_len, QB_W), 0)
    lane_k = lax.broadcasted_iota(jnp.int32, (t_len, QB_W), 1) // DK_B
    lane_v = lax.broadcasted_iota(jnp.int32, (t_len, VB_W), 1) // DV_B

    g_hi = g.astype(BF16)
    g_lo = (g - g_hi.astype(F32)).astype(BF16)
    tri = tri_ref[0:t_len, 0:t_len]
    b = (jnp.dot(tri, g_hi, preferred_element_type=F32)
         + jnp.dot(tri, g_lo, preferred_element_type=F32))
    b_last = b[t_len - 1:t_len, :]

    o = lax.dot_general((q * jnp.exp(b)).astype(BF16), st.astype(BF16), nt, preferred_element_type=F32)

    nodes = _gla_nodes(t_len)
    if nodes:
        qf_parts, kf_parts = [], []
        for (q_lo, q_hi, k_lo, k_hi, a) in nodes:
            anchor = b[a:a + 1, :]
            qf = jnp.where((row >= q_lo) & (row < q_hi), q * jnp.exp(jnp.minimum(b - anchor, 0.0)), 0.0)
            kf = jnp.where((row >= k_lo) & (row < k_hi), k * jnp.exp(jnp.minimum(anchor - b, 0.0)), 0.0)
            qf_parts.append(jnp.concatenate(
                [jnp.where(lane_k == hb, qf, 0.0) for hb in range(H_B)], axis=0).astype(BF16))
            kf_parts.append(kf.astype(BF16))
        q_all = qf_parts[0] if len(nodes) == 1 else jnp.concatenate(qf_parts, axis=1)
        k_all = kf_parts[0] if len(nodes) == 1 else jnp.concatenate(kf_parts, axis=1)
        att = lax.dot_general(q_all, k_all, nt, preferred_element_type=F32)
        full = jnp.dot(att.astype(BF16), v.astype(BF16), preferred_element_type=F32)
        for hb in range(H_B):
            o = o + jnp.where(lane_v == hb, full[hb * t_len:(hb + 1) * t_len], 0.0)

    e_parts, v_rolls = [], []
    for d in range(SUB_BLOCK):
        ok = (row % SUB_BLOCK) >= d
        k_d = k if d == 0 else pltpu.roll(k, d, 0)
        b_d = b if d == 0 else pltpu.roll(b, d, 0)
        v_rolls.append(v if d == 0 else pltpu.roll(v, d, 0))
        e_parts.append(jnp.where(ok, q * k_d * jnp.exp(jnp.where(ok, b - b_d, 0.0)), 0.0).astype(BF16))
    coef = jnp.dot(jnp.concatenate(e_parts, axis=0), ones_ref[...], preferred_element_type=F32)
    for d in range(SUB_BLOCK):
        o = o + coef[d * t_len:(d + 1) * t_len] * v_rolls[d]

    kf = (k * jnp.exp(b_last - b)).astype(BF16)
    upd = lax.dot_general(v.astype(BF16), kf, (((0,), (0,)), ((), ())), preferred_element_type=F32)
    st_new = jnp.where(stmask_ref[...] > 0.0, jnp.exp(b_last) * st + upd, 0.0)
    return o, st_new


def _gla_prompt_body(qkg_ref, v_ref, tri_ref, ones_ref, stmask_ref, o_ref, so_ref, st_ref):
    seq = qkg_ref.shape[0]

    def run(r0, t_len):
        rows = pl.ds(r0, t_len)
        o, st = _gla_chunk(qkg_ref[rows, 0:QB_W], qkg_ref[rows, QB_W:2 * QB_W],
                           qkg_ref[rows, 2 * QB_W:3 * QB_W], v_ref[rows, :], st_ref[...],
                           tri_ref, ones_ref, stmask_ref)
        o_ref[rows, :] = o
        st_ref[...] = st

    st_ref[...] = jnp.zeros(st_ref.shape, F32)
    run(0, N_META)

    def chunk(c, carry):
        run(pl.multiple_of(N_META + c * CHUNK, 8), CHUNK)
        return carry

    lax.fori_loop(0, (seq - N_META) // CHUNK, chunk, 0)
    so_ref[0] = st_ref[...]


def _gla_consts():
    i = np.arange(CHUNK)
    tri = (i[:, None] >= i[None, :]).astype(np.float32)
    kd = np.arange(QB_W) // DK_B
    vd = np.arange(VB_W) // DV_B
    ones = (kd[:, None] == vd[None, :]).astype(np.float32)
    stmask = (vd[:, None] == kd[None, :]).astype(np.float32)
    return jnp.asarray(tri, BF16), jnp.asarray(ones, BF16), jnp.asarray(stmask, F32)


def _gla_prompt(qkg, vb, *, batch):
    m = qkg.shape[0]
    seq = m // batch
    tri, ones, stmask = _gla_consts()
    return pl.pallas_call(
        _gla_prompt_body,
        grid=(batch,),
        in_specs=[pl.BlockSpec((seq, 3 * QB_W), lambda b: (b, 0)),
                  pl.BlockSpec((seq, VB_W), lambda b: (b, 0)),
                  _const_spec(tri.shape), _const_spec(ones.shape), _const_spec(stmask.shape)],
        out_specs=[pl.BlockSpec((seq, VB_W), lambda b: (b, 0)),
                   pl.BlockSpec((1, VB_W, QB_W), lambda b: (b, 0, 0))],
        out_shape=[jax.ShapeDtypeStruct((m, VB_W), F32),
                   jax.ShapeDtypeStruct((batch, VB_W, QB_W), F32)],
        scratch_shapes=[pltpu.VMEM((VB_W, QB_W), F32)],
        compiler_params=_cparams(1),
        name="gla_prompt",
    )(qkg, vb, tri, ones, stmask)


def _gla_sample_body(qkg_ref, v_ref, si_ref, tri_ref, ones_ref, stmask_ref, o_ref, so_ref):
    o, st = _gla_chunk(qkg_ref[:, 0:QB_W], qkg_ref[:, QB_W:2 * QB_W], qkg_ref[:, 2 * QB_W:3 * QB_W],
                       v_ref[...], si_ref[0], tri_ref, ones_ref, stmask_ref)
    o_ref[...] = o
    so_ref[0] = st


def _gla_sample(qkg, vb, state_t, *, batch):
    m = qkg.shape[0]
    t = m // batch
    tri, ones, stmask = _gla_consts()
    st_spec = pl.BlockSpec((1, VB_W, QB_W), lambda b: (b, 0, 0))
    return pl.pallas_call(
        _gla_sample_body,
        grid=(batch,),
        in_specs=[pl.BlockSpec((t, 3 * QB_W), lambda b: (b, 0)), pl.BlockSpec((t, VB_W), lambda b: (b, 0)),
                  st_spec, _const_spec(tri.shape), _const_spec(ones.shape), _const_spec(stmask.shape)],
        out_specs=[pl.BlockSpec((t, VB_W), lambda b: (b, 0)), st_spec],
        out_shape=[jax.ShapeDtypeStruct((m, VB_W), F32),
                   jax.ShapeDtypeStruct((batch, VB_W, QB_W), F32)],
        compiler_params=_cparams(1),
        name="gla_sample",
    )(qkg, vb, state_t, tri, ones, stmask)


def _state_to_blockdiag(s):
    bsz = s.shape[0]
    eye = jnp.eye(H_B, dtype=s.dtype)
    st = jnp.einsum('bhdv,hg->bhvgd', s, eye)
    return st.reshape(bsz, VB_W, QB_W)


def _state_from_blockdiag(st):
    bsz = st.shape[0]
    st5 = st.reshape(bsz, H_B, DV_B, H_B, DK_B)
    diag = jnp.stack([st5[:, hb, :, hb, :] for hb in range(H_B)], axis=1)
    return diag.transpose(0, 1, 3, 2)


FOX_TILE = MXU_DIM


def _cumsum_rows(x, tri_ref, carry):
    n = x.shape[0]
    tri = tri_ref[0:n, 0:n]
    hi = x.astype(BF16)
    r1 = x - hi.astype(F32)
    mid = r1.astype(BF16)
    lo = (r1 - mid.astype(F32)).astype(BF16)
    c = (jnp.dot(tri, hi, preferred_element_type=F32) + jnp.dot(tri, mid, preferred_element_type=F32)
         + jnp.dot(tri, lo, preferred_element_type=F32)) + carry
    return c, c[n - 1:n, :]


def _fox_aux(c):
    grp = (lax.broadcasted_iota(jnp.int32, c.shape, 1) % HEAD_DIM) // H_C
    c1 = c.astype(BF16)
    r1 = c - c1.astype(F32)
    c2 = r1.astype(BF16)
    c3 = (r1 - c2.astype(F32)).astype(BF16)
    zero = jnp.zeros(c.shape, BF16)
    ak = jnp.where(grp == 0, -c1, jnp.where(grp == 1, -c2, jnp.where(grp == 2, -c3, zero)))
    aq = jnp.where(grp == 3, c1, zero)
    return ak, aq


def _fox_operand(x, aux, h, head_id, key_side):
    lane = lax.broadcasted_iota(jnp.int32, x.shape, 1)
    own = (lane // HEAD_DIM) == h
    sub = lane % HEAD_DIM
    if key_side:
        sel = sub == (3 * H_C + head_id)
    else:
        sel = ((sub % H_C) == head_id) & (sub < 3 * H_C)
    one = jnp.ones(x.shape, BF16)
    return jnp.where(own, x, jnp.where(sel, one, aux))


def _fox_update(s, m, l, acc, vt, pad_keys=0):
    m_new = jnp.maximum(m, jnp.max(s, axis=0, keepdims=True))
    alpha = jnp.exp(m - m_new)
    p = jnp.exp(s - m_new)
    l = alpha * l + jnp.sum(p, axis=0, keepdims=True)
    pb = p.astype(BF16)
    if pad_keys:
        pb = jnp.concatenate([pb, jnp.zeros((pad_keys, pb.shape[1]), BF16)], axis=0)
    acc = alpha * acc + jnp.dot(vt, pb, preferred_element_type=F32)
    return m_new, l, acc


def _fox_scores(kaug, qaug):
    return lax.dot_general(kaug, qaug, (((1,), (1,)), ((), ())), preferred_element_type=F32)


def _causal(s):
    key = lax.broadcasted_iota(jnp.int32, s.shape, 0)
    qry = lax.broadcasted_iota(jnp.int32, s.shape, 1)
    return jnp.where(key <= qry, s, NEG)


def _fox_init(tq):
    return (jnp.full((1, tq), NEG, F32), jnp.zeros((1, tq), F32), jnp.zeros((HEAD_DIM, tq), F32))


def _fox_finish(state0, state1):
    ot = jnp.concatenate([state0[2] / state0[1], state1[2] / state1[1]], axis=0)
    return ot.T


def _fox_prompt_body(q_ref, k_ref, v_ref, lf_ref, tri_ref, o_ref,
                     ak_ref, aq_ref, kaug_ref, vt_ref, vtm_ref):
    hp = pl.program_id(1)
    seq = q_ref.shape[0]
    n_tiles = (seq - N_META) // FOX_TILE

    @pl.when(hp == 0)
    def _():
        c, carry = _cumsum_rows(lf_ref[0:N_META, :], tri_ref, jnp.zeros((1, LANES), F32))
        ak, aq = _fox_aux(c)
        ak_ref[0:N_META, :] = ak
        aq_ref[0:N_META, :] = aq
        for j in range(n_tiles):
            r0 = N_META + j * FOX_TILE
            c, carry = _cumsum_rows(lf_ref[r0:r0 + FOX_TILE, :], tri_ref, carry)
            ak, aq = _fox_aux(c)
            ak_ref[r0:r0 + FOX_TILE, :] = ak
            aq_ref[r0:r0 + FOX_TILE, :] = aq

    heads = [2 * hp, 2 * hp + 1]
    kb = k_ref[...].astype(BF16)
    for h in range(2):
        kaug_ref[h] = _fox_operand(kb, ak_ref[...], h, heads[h], True)
    lane_m = lax.broadcasted_iota(jnp.int32, (LANES, LANES), 1)
    vtm_ref[...] = jnp.where(lane_m < N_META, v_ref[0:LANES, :].T, 0.0).astype(BF16)
    for j in range(n_tiles):
        r0 = N_META + j * FOX_TILE
        vt_ref[j] = v_ref[r0:r0 + FOX_TILE, :].T.astype(BF16)

    def q_operands(r0, tq):
        qt = q_ref[pl.ds(r0, tq), :]
        aqt = aq_ref[pl.ds(r0, tq), :]
        return [_fox_operand(qt, aqt, h, heads[h], False) for h in range(2)]

    def meta_step(qaug, causal):
        states = []
        for h in range(2):
            s = _fox_scores(kaug_ref[h, 0:N_META, :], qaug[h])
            if causal:
                s = _causal(s)
            states.append(_fox_update(s, *_fox_init(qaug[h].shape[0]),
                                      vtm_ref[h * HEAD_DIM:(h + 1) * HEAD_DIM, :], LANES - N_META))
        return states

    st = meta_step(q_operands(0, LANES), True)
    o_ref[0:N_META, :] = _fox_finish(st[0], st[1])[0:N_META, :].astype(o_ref.dtype)

    def q_tile(i, carry):
        r0 = pl.multiple_of(N_META + i * FOX_TILE, 16)
        qaug = q_operands(r0, FOX_TILE)
        st = meta_step(qaug, False)

        def kv_tile(j, flat):
            k0 = pl.multiple_of(N_META + j * FOX_TILE, 16)
            out = []
            for h in range(2):
                s = _fox_scores(kaug_ref[h, pl.ds(k0, FOX_TILE), :], qaug[h])
                out.extend(_fox_update(s, *flat[3 * h:3 * h + 3],
                                       vt_ref[j, h * HEAD_DIM:(h + 1) * HEAD_DIM, :]))
            return tuple(out)

        flat = lax.fori_loop(0, i, kv_tile, tuple(st[0]) + tuple(st[1]))
        fin = []
        for h in range(2):
            s = _causal(_fox_scores(kaug_ref[h, pl.ds(r0, FOX_TILE), :], qaug[h]))
            fin.append(_fox_update(s, *flat[3 * h:3 * h + 3], vt_ref[i, h * HEAD_DIM:(h + 1) * HEAD_DIM, :]))
        o_ref[pl.ds(r0, FOX_TILE), :] = _fox_finish(fin[0], fin[1]).astype(o_ref.dtype)
        return carry

    lax.fori_loop(0, n_tiles, q_tile, 0)


def _fox_tri():
    i = np.arange(FOX_TILE)
    return jnp.asarray((i[:, None] >= i[None, :]).astype(np.float32), BF16)


def _fox_prompt(q, k, v, lf, *, batch):
    m = q.shape[0]
    seq = m // batch
    n_tiles = (seq - N_META) // FOX_TILE
    tri = _fox_tri()
    col = pl.BlockSpec((seq, LANES), lambda b, hp: (b, hp))
    return pl.pallas_call(
        _fox_prompt_body,
        grid=(batch, H_C // 2),
        in_specs=[col, col, col, pl.BlockSpec((seq, LANES), lambda b, hp: (b, 0)), _const_spec(tri.shape)],
        out_specs=col,
        out_shape=jax.ShapeDtypeStruct((m, C_W), BF16),
        scratch_shapes=[pltpu.VMEM((seq, LANES), BF16), pltpu.VMEM((seq, LANES), BF16),
                        pltpu.VMEM((2, seq, LANES), BF16),
                        pltpu.VMEM((n_tiles, LANES, FOX_TILE), BF16), pltpu.VMEM((LANES, LANES), BF16)],
        compiler_params=_cparams(2),
        name="fox_prompt",
    )(q, k, v, lf, tri)


def _fox_sample_body(q_ref, k_ref, v_ref, lf_ref, ck_ref, cv_ref, clf_ref, tri_ref, o_ref,
                     ak_ref, aq_ref, kaug_ref, vt_ref, vtn_ref):
    hp = pl.program_id(1)
    t = q_ref.shape[0]
    past = ck_ref.shape[1]
    n_tiles = past // FOX_TILE

    @pl.when(hp == 0)
    def _():
        carry = jnp.zeros((1, LANES), F32)
        for j in range(n_tiles):
            r0 = j * FOX_TILE
            c, carry = _cumsum_rows(clf_ref[0, r0:r0 + FOX_TILE, :], tri_ref, carry)
            ak_ref[r0:r0 + FOX_TILE, :] = _fox_aux(c)[0]
        c, carry = _cumsum_rows(lf_ref[...], tri_ref, carry)
        ak, aq = _fox_aux(c)
        ak_ref[past:past + t, :] = ak
        aq_ref[0:t, :] = aq
        aq_ref[t:LANES, :] = jnp.zeros((LANES - t, LANES), BF16)

    heads = [2 * hp, 2 * hp + 1]
    kb = jnp.concatenate([ck_ref[0].astype(BF16), k_ref[...].astype(BF16)], axis=0)
    for h in range(2):
        kaug_ref[h] = _fox_operand(kb, ak_ref[...], h, heads[h], True)
    for j in range(n_tiles):
        vt_ref[j] = cv_ref[0, j * FOX_TILE:(j + 1) * FOX_TILE, :].T.astype(BF16)
    vn = jnp.concatenate([v_ref[...], jnp.zeros((LANES - t, LANES), F32)], axis=0)
    vtn_ref[...] = vn.T.astype(BF16)

    qt = jnp.concatenate([q_ref[...], jnp.zeros((LANES - t, LANES), BF16)], axis=0)
    qaug = [_fox_operand(qt, aq_ref[...], h, heads[h], False) for h in range(2)]

    def kv_tile(j, flat):
        k0 = pl.multiple_of(j * FOX_TILE, FOX_TILE)
        out = []
        for h in range(2):
            s = _fox_scores(kaug_ref[h, pl.ds(k0, FOX_TILE), :], qaug[h])
            out.extend(_fox_update(s, *flat[3 * h:3 * h + 3], vt_ref[j, h * HEAD_DIM:(h + 1) * HEAD_DIM, :]))
        return tuple(out)

    flat = lax.fori_loop(0, n_tiles, kv_tile, _fox_init(LANES) + _fox_init(LANES))
    fin = []
    for h in range(2):
        s = _causal(_fox_scores(kaug_ref[h, past:past + t, :], qaug[h]))
        fin.append(_fox_update(s, *flat[3 * h:3 * h + 3], vtn_ref[h * HEAD_DIM:(h + 1) * HEAD_DIM, :], LANES - t))
    o_ref[...] = _fox_finish(fin[0], fin[1])[0:t, :].astype(o_ref.dtype)


def _fox_sample(q, k, v, lf, cache_k, cache_v, cache_lf, *, batch):
    m = q.shape[0]
    t = m // batch
    past = cache_k.shape[1]
    tri = _fox_tri()
    col = pl.BlockSpec((t, LANES), lambda b, hp: (b, hp))
    ccol = pl.BlockSpec((1, past, LANES), lambda b, hp: (b, 0, hp))
    return pl.pallas_call(
        _fox_sample_body,
        grid=(batch, H_C // 2),
        in_specs=[col, col, col, pl.BlockSpec((t, LANES), lambda b, hp: (b, 0)), ccol, ccol,
                  pl.BlockSpec((1, past, LANES), lambda b, hp: (b, 0, 0)), _const_spec(tri.shape)],
        out_specs=col,
        out_shape=jax.ShapeDtypeStruct((m, C_W), BF16),
        scratch_shapes=[pltpu.VMEM((past + t, LANES), BF16), pltpu.VMEM((LANES, LANES), BF16),
                        pltpu.VMEM((2, past + t, LANES), BF16),
                        pltpu.VMEM((past // FOX_TILE, LANES, FOX_TILE), BF16),
                        pltpu.VMEM((LANES, LANES), BF16)],
        compiler_params=_cparams(2),
        name="fox_sample",
    )(q, k, v, lf, cache_k, cache_v, cache_lf, tri)


def _head_block_diag():
    i = np.arange(MXU_DIM)
    same = (i[:, None] // HEAD_DIM) == (i[None, :] // HEAD_DIM)
    return jnp.asarray(same.astype(np.float32) / HEAD_DIM, dtype=BF16)


def _rope_tables(pos):
    half = HEAD_DIM // 2
    inv = ROPE_THETA ** (-jnp.arange(half, dtype=F32) / half)
    ang = pos.astype(F32)[:, None] * inv[None, :]
    cos = jnp.cos(ang)
    sin = jnp.sin(ang)
    cos64 = jnp.concatenate([cos, cos], axis=1)
    sin64 = jnp.concatenate([-sin, sin], axis=1)
    return jnp.tile(cos64, (1, LANES // HEAD_DIM)), jnp.tile(sin64, (1, LANES // HEAD_DIM))


def _prep_ab(w_in, qn, kn, w_gate, b_gate):
    w_main = w_in[:, :AB_MAIN].astype(BF16)
    w_rank = jnp.pad(w_in[:, AB_MAIN:], ((0, 0), (0, LANES - GATE_RANK))).astype(BF16)
    wgate = jnp.pad(w_gate, ((0, LANES - GATE_RANK), (0, 0))).astype(BF16)
    return (w_main, w_rank, jnp.tile(qn, H_A)[None, :], jnp.tile(kn, KV_A)[None, :],
            wgate, b_gate[None, :])


def _prep_c(w_in, b_f, qn, kn):
    w_main = w_in[:, :3 * C_W].astype(BF16)
    rep = LANES // H_C
    w_f = jnp.tile(w_in[:, 3 * C_W:], (1, rep)).astype(BF16)
    return (w_main, w_f, jnp.tile(b_f, rep)[None, :],
            jnp.tile(qn, MXU_DIM // HEAD_DIM)[None, :], jnp.tile(kn, MXU_DIM // HEAD_DIM)[None, :])


PROMPT_TILES_PER_STREAM = 3


def kernel(x_prompt, x_sample, cache_a_k, cache_a_v, state_b, cache_c_k, cache_c_v, cache_c_logf,
           meta_tokens, norm_mix, norm_mlp, w_in_ab, qnorm_a, knorm_a, sink_a, w_gate_b, b_gate_b,
           onorm_b, w_out_ab, w_in_c, b_f_c, qnorm_c, knorm_c, w_out_c, w_up, w_down):
    bp_, seq_in = x_prompt.shape[:2]
    bs_, t_new = x_sample.shape[:2]
    seq = N_META + seq_in
    past = cache_c_k.shape[2]
    depth = norm_mix.shape[0]
    tm_p = seq // PROMPT_TILES_PER_STREAM
    tm_s = bs_ * t_new

    meta = jnp.broadcast_to(meta_tokens.astype(x_prompt.dtype)[None], (bp_, N_META, D_MODEL))
    xp = jnp.concatenate([meta, x_prompt], axis=1).reshape(bp_ * seq, D_MODEL)
    xs = x_sample.reshape(bs_ * t_new, D_MODEL)

    bd = _head_block_diag()
    cos_p, sin_p = _rope_tables(jnp.arange(seq))
    cos_s, sin_s = _rope_tables(N_META + past + jnp.arange(t_new))
    cos_s, sin_s = jnp.tile(cos_s, (bs_, 1)), jnp.tile(sin_s, (bs_, 1))

    akp, avp, bpo, ckp, cvp, cfp = [], [], [], [], [], []
    aks, avs, bso, cks, cvs, cfs = [], [], [], [], [], []
    for l in range(depth):
        i = l // 2
        g_mix = norm_mix[l][None, :]
        g_mlp = norm_mlp[l][None, :]
        wup = w_up[l].astype(BF16)
        wdown = w_down[l].astype(BF16)
        if l % 2 == 0:
            w_main, w_rank, qn, kn, wgate, bgate = _prep_ab(
                w_in_ab[i], qnorm_a[i], knorm_a[i], w_gate_b[i], b_gate_b[i])
            wout = w_out_ab[i].astype(BF16)
            onorm = onorm_b[i][None, :]

            qa, kva, qkg, vb, rb = _proj_ab(xp, g_mix, w_main, w_rank, bd, qn, kn, cos_p, sin_p, wgate, bgate,
                                            tm=tm_p, n_pos_tiles=PROMPT_TILES_PER_STREAM)
            oa = _swa_prompt(sink_a[i], qa, kva, batch=bp_)
            ob, st = _gla_prompt(qkg, vb, batch=bp_)
            xp = _post_ab(xp, oa, ob, rb, onorm, wout, g_mlp, wup, wdown, tm=tm_p)
            kv3 = kva.reshape(bp_, seq, 2 * KA_W)[:, seq - WINDOW:, :]
            akp.append(kv3[:, :, 0:KA_W].reshape(bp_, WINDOW, KV_A, HEAD_DIM))
            avp.append(kv3[:, :, KA_W:].reshape(bp_, WINDOW, KV_A, HEAD_DIM))
            bpo.append(_state_from_blockdiag(st))

            qa, kva, qkg, vb, rb = _proj_ab(xs, g_mix, w_main, w_rank, bd, qn, kn, cos_s, sin_s, wgate, bgate,
                                            tm=tm_s, n_pos_tiles=1)
            oa, nk, nv = _swa_sample(sink_a[i], qa, kva, cache_a_k[i].reshape(bs_, WINDOW, KA_W),
                                     cache_a_v[i].reshape(bs_, WINDOW, KA_W), batch=bs_)
            ob, st = _gla_sample(qkg, vb, _state_to_blockdiag(state_b[i]), batch=bs_)
            xs = _post_ab(xs, oa, ob, rb, onorm, wout, g_mlp, wup, wdown, tm=tm_s)
            aks.append(nk.reshape(bs_, WINDOW, KV_A, HEAD_DIM))
            avs.append(nv.reshape(bs_, WINDOW, KV_A, HEAD_DIM))
            bso.append(_state_from_blockdiag(st))
        else:
            w_main, w_f, b_f, qn, kn = _prep_c(w_in_c[i], b_f_c[i], qnorm_c[i], knorm_c[i])
            wout = w_out_c[i].astype(BF16)

            q, k, v, lf = _proj_c(xp, g_mix, w_main, w_f, b_f, bd, qn, kn, tm=tm_p)
            o = _fox_prompt(q, k, v, lf, batch=bp_)
            xp = _post_c(xp, o, wout, g_mlp, wup, wdown, tm=tm_p)
            ckp.append(k.reshape(bp_, seq, H_C, HEAD_DIM))
            cvp.append(v.reshape(bp_, seq, H_C, HEAD_DIM))
            cfp.append(lf[:, 0:H_C].reshape(bp_, seq, H_C))

            q, k, v, lf = _proj_c(xs, g_mix, w_main, w_f, b_f, bd, qn, kn, tm=tm_s)
            o = _fox_sample(q, k, v, lf, cache_c_k[i].reshape(bs_, past, C_W),
                            cache_c_v[i].reshape(bs_, past, C_W),
                            jnp.tile(cache_c_logf[i], (1, 1, LANES // H_C)), batch=bs_)
            xs = _post_c(xs, o, wout, g_mlp, wup, wdown, tm=tm_s)
            cks.append(k.reshape(bs_, t_new, H_C, HEAD_DIM))
            cvs.append(v.reshape(bs_, t_new, H_C, HEAD_DIM))
            cfs.append(lf[:, 0:H_C].reshape(bs_, t_new, H_C))

    y_prompt = xp.reshape(bp_, seq, D_MODEL)[:, N_META:]
    y_sample = xs.reshape(bs_, t_new, D_MODEL)
    return (y_prompt, y_sample,
            jnp.stack(akp), jnp.stack(avp), jnp.stack(bpo), jnp.stack(ckp), jnp.stack(cvp), jnp.stack(cfp),
            jnp.stack(aks), jnp.stack(avs), jnp.stack(bso), jnp.stack(cks), jnp.stack(cvs), jnp.stack(cfs))
```

```python
import functools

import numpy as np
import jax
import jax.numpy as jnp
from jax import lax
from jax.experimental import pallas as pl
from jax.experimental.pallas import tpu as pltpu

F32 = jnp.float32
BF16 = jnp.bfloat16

D_MODEL = 1024
CHUNK = 64
N_META = 16
HEAD_DIM = 64
H_A = 8
KV_A = 2
G_A = H_A // KV_A
WINDOW = 128
H_B = 4
DK_B = 64
DV_B = 128
GATE_RANK = 16
GATE_TAU = 16.0
H_C = 16
D_FF = 4 * D_MODEL
ROPE_THETA = 10000.0
EPS = 1e-6
NEG = -1e30
SCALE = HEAD_DIM ** -0.5
LOG2E = 1.4426950408889634
QA_W = H_A * HEAD_DIM
KA_W = KV_A * HEAD_DIM
QB_W = H_B * DK_B
VB_W = H_B * DV_B
AB_MAIN = QA_W + 2 * KA_W + 2 * QB_W + 2 * VB_W
C_W = H_C * HEAD_DIM

LANES = 128
MXU_DIM = 256
VMEM_LIMIT = 56 * 1024 * 1024
SUB_BLOCK = 16


def _cparams(n_axes):
    return pltpu.CompilerParams(
        dimension_semantics=("arbitrary",) * n_axes, vmem_limit_bytes=VMEM_LIMIT)


def _const_spec(shape):
    nd = len(shape)
    return pl.BlockSpec(shape, lambda *_: (0,) * nd, pipeline_mode=pl.Buffered(1))


def _rmsnorm_rows(x, g):
    ms = jnp.mean(x * x, axis=-1, keepdims=True)
    return x * lax.rsqrt(ms + EPS) * g


def _log_sigmoid(x):
    return jnp.minimum(x, 0.0) - jnp.log1p(jnp.exp(-jnp.abs(x)))


def _head_mean_square(x, bd_ref):
    xx = (x * x).astype(BF16)
    w = x.shape[1]
    parts = []
    for s in range(0, w, MXU_DIM):
        e = min(s + MXU_DIM, w)
        parts.append(jnp.dot(xx[:, s:e], bd_ref[0:e - s, 0:e - s], preferred_element_type=F32))
    return parts[0] if len(parts) == 1 else jnp.concatenate(parts, axis=1)


def _rotate_half(x):
    w = x.shape[1]
    lane = lax.broadcasted_iota(jnp.int32, x.shape, 1)
    first = (lane % HEAD_DIM) < (HEAD_DIM // 2)
    return jnp.where(first, pltpu.roll(x, w - HEAD_DIM // 2, 1), pltpu.roll(x, HEAD_DIM // 2, 1))


def _proj_ab_body(x_ref, g_ref, w_ref, bd_ref, qn_ref, kn_ref, cos_ref, sin_ref,
                  wgate_ref, bgate_ref, qa_ref, kva_ref, qkg_ref, vb_ref, rb_ref):
    h = _rmsnorm_rows(x_ref[...], g_ref[...]).astype(BF16)

    z = jnp.dot(h, w_ref[:, 0:QA_W + 2 * KA_W], preferred_element_type=F32)
    cos = cos_ref[...]
    sin = sin_ref[...]

    qa = z[:, 0:QA_W]
    qa = qa * lax.rsqrt(_head_mean_square(qa, bd_ref) + EPS) * qn_ref[...]
    cos_q = jnp.concatenate([cos] * (QA_W // LANES), axis=1)
    sin_q = jnp.concatenate([sin] * (QA_W // LANES), axis=1)
    qa_ref[...] = (qa * cos_q + _rotate_half(qa) * sin_q).astype(qa_ref.dtype)

    ka = z[:, QA_W:QA_W + KA_W]
    ka = ka * lax.rsqrt(_head_mean_square(ka, bd_ref) + EPS) * kn_ref[...]
    kva_ref[:, 0:KA_W] = ka * cos + _rotate_half(ka) * sin
    kva_ref[:, KA_W:2 * KA_W] = z[:, QA_W + KA_W:QA_W + 2 * KA_W]

    c0 = QA_W + 2 * KA_W
    z = jnp.dot(h, w_ref[:, c0:c0 + 2 * QB_W], preferred_element_type=F32)
    qkg_ref[:, 0:QB_W] = z[:, 0:QB_W] * (DK_B ** -0.5)
    qkg_ref[:, QB_W:2 * QB_W] = z[:, QB_W:2 * QB_W]

    c1 = c0 + 2 * QB_W
    vb_ref[...] = jnp.dot(h, w_ref[:, c1:c1 + VB_W], preferred_element_type=F32)
    z = jnp.dot(h, w_ref[:, c1 + VB_W:c1 + 2 * VB_W + LANES], preferred_element_type=F32)
    rb_ref[...] = z[:, 0:VB_W]
    glr = z[:, VB_W:VB_W + LANES]
    gl = jnp.dot(glr.astype(BF16), wgate_ref[...], preferred_element_type=F32) + bgate_ref[...]
    qkg_ref[:, 2 * QB_W:3 * QB_W] = _log_sigmoid(gl) * (LOG2E / GATE_TAU)


def _proj_ab(x, g, w_main, bd, qn, kn, cos, sin, wgate, bgate, *, tm, n_pos_tiles):
    m = x.shape[0]
    row = lambda w: pl.BlockSpec((tm, w), lambda i: (i, 0))
    pos = pl.BlockSpec((tm, LANES), lambda i: (i % n_pos_tiles, 0))
    return pl.pallas_call(
        _proj_ab_body,
        grid=(m // tm,),
        in_specs=[row(D_MODEL), _const_spec((1, D_MODEL)), _const_spec(w_main.shape),
                  _const_spec(bd.shape), _const_spec(qn.shape),
                  _const_spec(kn.shape), pos, pos, _const_spec(wgate.shape), _const_spec(bgate.shape)],
        out_specs=[row(QA_W), row(2 * KA_W), row(3 * QB_W), row(VB_W), row(VB_W)],
        out_shape=[jax.ShapeDtypeStruct((m, QA_W), BF16), jax.ShapeDtypeStruct((m, 2 * KA_W), F32),
                   jax.ShapeDtypeStruct((m, 3 * QB_W), F32), jax.ShapeDtypeStruct((m, VB_W), F32),
                   jax.ShapeDtypeStruct((m, VB_W), F32)],
        compiler_params=_cparams(1),
        name="proj_ab",
    )(x, g, w_main, bd, qn, kn, cos, sin, wgate, bgate)


def _proj_c_body(x_ref, g_ref, w_ref, wf_ref, bf_ref, bd_ref, qn_ref, kn_ref,
                 q_ref, k_ref, v_ref, lf_ref):
    h = _rmsnorm_rows(x_ref[...], g_ref[...]).astype(BF16)
    z = jnp.dot(h, w_ref[:, 0:2 * C_W], preferred_element_type=F32)
    v_ref[...] = jnp.dot(h, w_ref[:, 2 * C_W:3 * C_W], preferred_element_type=F32)
    for s in range(0, C_W, MXU_DIM):
        zq = z[:, s:s + MXU_DIM]
        zq = zq * lax.rsqrt(_head_mean_square(zq, bd_ref) + EPS) * qn_ref[...]
        q_ref[:, s:s + MXU_DIM] = (zq * (SCALE * LOG2E)).astype(q_ref.dtype)
        zk = z[:, C_W + s:C_W + s + MXU_DIM]
        k_ref[:, s:s + MXU_DIM] = zk * lax.rsqrt(_head_mean_square(zk, bd_ref) + EPS) * kn_ref[...]
    fl = jnp.dot(h, wf_ref[...], preferred_element_type=F32) + bf_ref[...]
    lf_ref[...] = _log_sigmoid(fl)


def _proj_c(x, g, w_main, w_f, b_f, bd, qn, kn, *, tm):
    m = x.shape[0]
    row = lambda w: pl.BlockSpec((tm, w), lambda i: (i, 0))
    return pl.pallas_call(
        _proj_c_body,
        grid=(m // tm,),
        in_specs=[row(D_MODEL), _const_spec((1, D_MODEL)), _const_spec(w_main.shape),
                  _const_spec(w_f.shape), _const_spec(b_f.shape), _const_spec(bd.shape),
                  _const_spec(qn.shape), _const_spec(kn.shape)],
        out_specs=[row(C_W), row(C_W), row(C_W), row(LANES)],
        out_shape=[jax.ShapeDtypeStruct((m, C_W), BF16), jax.ShapeDtypeStruct((m, C_W), F32),
                   jax.ShapeDtypeStruct((m, C_W), F32), jax.ShapeDtypeStruct((m, LANES), F32)],
        compiler_params=_cparams(1),
        name="proj_c",
    )(x, g, w_main, w_f, b_f, bd, qn, kn)


def _mlp_tail(x1, g2_ref, wup_ref, wdown_ref, out_ref):
    h2 = _rmsnorm_rows(x1, g2_ref[...]).astype(BF16)
    acc = x1
    for c in range(0, D_FF, D_MODEL):
        u = jnp.dot(h2, wup_ref[:, c:c + D_MODEL], preferred_element_type=F32)
        u = jnp.maximum(u, 0.0)
        acc = acc + jnp.dot((u * u).astype(BF16), wdown_ref[c:c + D_MODEL, :], preferred_element_type=F32)
    out_ref[...] = acc


def _post_ab_body(x_ref, oa_ref, ob_ref, rb_ref, on_ref, wout_ref, g2_ref, wup_ref, wdown_ref, out_ref):
    ob = ob_ref[...]
    parts = []
    for hb in range(H_B):
        sl = ob[:, hb * DV_B:(hb + 1) * DV_B]
        parts.append(sl * lax.rsqrt(jnp.mean(sl * sl, axis=-1, keepdims=True) + EPS) * on_ref[...])
    rb = rb_ref[...]
    gate = rb / (1.0 + jnp.exp(-rb))
    obn = (jnp.concatenate(parts, axis=1) * gate).astype(BF16)
    y = jnp.dot(oa_ref[...], wout_ref[0:QA_W, :], preferred_element_type=F32)
    y = y + jnp.dot(obn, wout_ref[QA_W:QA_W + VB_W, :], preferred_element_type=F32)
    _mlp_tail(x_ref[...] + y, g2_ref, wup_ref, wdown_ref, out_ref)


def _post_c_body(x_ref, o_ref, wout_ref, g2_ref, wup_ref, wdown_ref, out_ref):
    y = jnp.dot(o_ref[...], wout_ref[...], preferred_element_type=F32)
    _mlp_tail(x_ref[...] + y, g2_ref, wup_ref, wdown_ref, out_ref)


def _post_ab(x, oa, ob, rb, onorm, wout, g2, wup, wdown, *, tm):
    m = x.shape[0]
    row = lambda w: pl.BlockSpec((tm, w), lambda i: (i, 0))
    return pl.pallas_call(
        _post_ab_body,
        grid=(m // tm,),
        in_specs=[row(D_MODEL), row(QA_W), row(VB_W), row(VB_W), _const_spec(onorm.shape),
                  _const_spec(wout.shape), _const_spec(g2.shape), _const_spec(wup.shape),
                  _const_spec(wdown.shape)],
        out_specs=row(D_MODEL),
        out_shape=jax.ShapeDtypeStruct((m, D_MODEL), F32),
        compiler_params=_cparams(1),
        name="post_ab",
    )(x, oa, ob, rb, onorm, wout, g2, wup, wdown)


def _post_c(x, o, wout, g2, wup, wdown, *, tm):
    m = x.shape[0]
    row = lambda w: pl.BlockSpec((tm, w), lambda i: (i, 0))
    return pl.pallas_call(
        _post_c_body,
        grid=(m // tm,),
        in_specs=[row(D_MODEL), row(C_W), _const_spec(wout.shape), _const_spec(g2.shape),
                  _const_spec(wup.shape), _const_spec(wdown.shape)],
        out_specs=row(D_MODEL),
        out_shape=jax.ShapeDtypeStruct((m, D_MODEL), F32),
        compiler_params=_cparams(1),
        name="post_c",
    )(x, o, wout, g2, wup, wdown)


def _post_c_frames(x, o, wout, g2, wup, wdown, *, batch, tm):
    seq = x.shape[0] // batch
    frames = seq - N_META
    per = frames // tm
    src = lambda w: pl.BlockSpec((pl.Element(tm), pl.Element(w)),
                                 lambda b, j: (pl.multiple_of(b * seq + N_META + j * tm, 8), 0))
    return pl.pallas_call(
        _post_c_body,
        grid=(batch, per),
        in_specs=[src(D_MODEL), src(C_W), _const_spec(wout.shape), _const_spec(g2.shape),
                  _const_spec(wup.shape), _const_spec(wdown.shape)],
        out_specs=pl.BlockSpec((tm, D_MODEL), lambda b, j: (b * per + j, 0)),
        out_shape=jax.ShapeDtypeStruct((batch * frames, D_MODEL), F32),
        compiler_params=_cparams(2),
        name="post_c_frames",
    )(x, o, wout, g2, wup, wdown)


SWA_FRONT = CHUNK - N_META + (WINDOW // CHUNK) * CHUNK
SWA_BAND = WINDOW + CHUNK
SWA_CHUNKS_PER_STEP = 4


def _dup_heads(x, hk):
    lane = lax.broadcasted_iota(jnp.int32, x.shape, 1)
    own = (lane < HEAD_DIM) if hk == 0 else (lane >= HEAD_DIM)
    return jnp.where(own, x, pltpu.roll(x, HEAD_DIM, 1))


def _swa_scores(qs, kband):
    tq = qs.shape[0]
    even = lax.broadcasted_iota(jnp.int32, (tq, LANES), 1) < HEAD_DIM
    zero = jnp.zeros((tq, LANES), qs.dtype)
    s0, s1 = qs[:, 0:LANES], qs[:, LANES:2 * LANES]
    qz = jnp.concatenate([jnp.where(even, s0, zero), jnp.where(even, zero, s0),
                          jnp.where(even, s1, zero), jnp.where(even, zero, s1)], axis=0)
    return lax.dot_general(qz, kband, (((1,), (1,)), ((), ())), preferred_element_type=F32) * SCALE


def _swa_attend(s, vband, valid, sinks):
    tq = s.shape[0] // G_A
    even = lax.broadcasted_iota(jnp.int32, (tq, LANES), 1) < HEAD_DIM
    if valid is not None:
        s = jnp.where(valid, s, NEG)
    rowh = lax.broadcasted_iota(jnp.int32, (G_A * tq, 1), 0) // tq
    sk = jnp.where(rowh == 0, sinks[0], jnp.where(rowh == 1, sinks[1],
                                                   jnp.where(rowh == 2, sinks[2], sinks[3])))
    m = jnp.maximum(jnp.max(s, axis=-1, keepdims=True), sk)
    p = jnp.exp(s - m)
    den = jnp.sum(p, axis=-1, keepdims=True) + jnp.exp(sk - m)
    o = jnp.dot(p.astype(BF16), vband, preferred_element_type=F32) / den
    return (jnp.where(even, o[0:tq], o[tq:2 * tq]),
            jnp.where(even, o[2 * tq:3 * tq], o[3 * tq:4 * tq]))


def _swa_prompt_body(sink_ref, q_ref, kv_ref, o_ref, kd_ref, vd_ref):
    seq = q_ref.shape[0]
    ka = kv_ref[:, 0:KA_W]
    va = kv_ref[:, KA_W:2 * KA_W]
    for hk in range(KV_A):
        kd_ref[hk, 0:SWA_FRONT, :] = jnp.zeros((SWA_FRONT, LANES), BF16)
        vd_ref[hk, 0:SWA_FRONT, :] = jnp.zeros((SWA_FRONT, LANES), BF16)
        kd_ref[hk, SWA_FRONT:SWA_FRONT + seq, :] = _dup_heads(ka, hk).astype(BF16)
        vd_ref[hk, SWA_FRONT:SWA_FRONT + seq, :] = _dup_heads(va, hk).astype(BF16)
    sinks = [[sink_ref[G_A * hk + g] for g in range(G_A)] for hk in range(KV_A)]

    def blocks(items, tq):
        work = [(r0, kb0, hk) for (r0, kb0) in items for hk in range(KV_A)]
        scores = [_swa_scores(q_ref[pl.ds(r0, tq), hk * G_A * HEAD_DIM:(hk + 1) * G_A * HEAD_DIM],
                              kd_ref[hk, pl.ds(kb0, SWA_BAND), :]) for (r0, kb0, hk) in work]
        col = lax.broadcasted_iota(jnp.int32, (G_A * tq, SWA_BAND), 1)
        for (r0, kb0, hk), s in zip(work, scores):
            o0, o1 = _swa_attend(s, vd_ref[hk, pl.ds(kb0, SWA_BAND), :], (col + kb0) >= SWA_FRONT, sinks[hk])
            c0 = hk * G_A * HEAD_DIM
            o_ref[pl.ds(r0, tq), c0:c0 + LANES] = o0.astype(o_ref.dtype)
            o_ref[pl.ds(r0, tq), c0 + LANES:c0 + 2 * LANES] = o1.astype(o_ref.dtype)

    blocks([(0, 0)], N_META)

    def step(it, carry):
        items = []
        for u in range(SWA_CHUNKS_PER_STEP):
            c = it * SWA_CHUNKS_PER_STEP + u
            items.append((pl.multiple_of(N_META + c * CHUNK, 16), pl.multiple_of((c + 1) * CHUNK, CHUNK)))
        blocks(items, CHUNK)
        return carry

    lax.fori_loop(0, (seq - N_META) // (CHUNK * SWA_CHUNKS_PER_STEP), step, 0)


def _swa_prompt(sink, qa, kva, *, batch):
    m = qa.shape[0]
    seq = m // batch
    assert (seq - N_META) % (CHUNK * SWA_CHUNKS_PER_STEP) == 0
    return pl.pallas_call(
        _swa_prompt_body,
        grid_spec=pltpu.PrefetchScalarGridSpec(
            num_scalar_prefetch=1,
            grid=(batch,),
            in_specs=[pl.BlockSpec((seq, QA_W), lambda b, s: (b, 0)),
                      pl.BlockSpec((seq, 2 * KA_W), lambda b, s: (b, 0))],
            out_specs=pl.BlockSpec((seq, QA_W), lambda b, s: (b, 0)),
            scratch_shapes=[pltpu.VMEM((KV_A, SWA_FRONT + seq, LANES), BF16),
                            pltpu.VMEM((KV_A, SWA_FRONT + seq, LANES), BF16)]),
        out_shape=jax.ShapeDtypeStruct((m, QA_W), BF16),
        compiler_params=_cparams(1),
        name="swa_prompt",
    )(sink, qa, kva)


def _swa_sample_body(sink_ref, q_ref, kv_ref, ck_ref, cv_ref, o_ref, nk_ref, nv_ref):
    t = q_ref.shape[0]
    kk = jnp.concatenate([ck_ref[0], kv_ref[:, 0:KA_W]], axis=0)
    vv = jnp.concatenate([cv_ref[0], kv_ref[:, KA_W:2 * KA_W]], axis=0)
    nk_ref[0] = kk[t:, :]
    nv_ref[0] = vv[t:, :]
    for hk in range(KV_A):
        sinks = [sink_ref[G_A * hk + g] for g in range(G_A)]
        c0 = hk * G_A * HEAD_DIM
        s = _swa_scores(q_ref[:, c0:c0 + G_A * HEAD_DIM], _dup_heads(kk, hk).astype(BF16))
        o0, o1 = _swa_attend(s, _dup_heads(vv, hk).astype(BF16), None, sinks)
        o_ref[:, c0:c0 + LANES] = o0.astype(o_ref.dtype)
        o_ref[:, c0 + LANES:c0 + 2 * LANES] = o1.astype(o_ref.dtype)


def _swa_sample(sink, qa, kva, cache_k, cache_v, *, batch):
    m = qa.shape[0]
    t = m // batch
    cache = pl.BlockSpec((1, WINDOW, KA_W), lambda b, s: (b, 0, 0))
    return pl.pallas_call(
        _swa_sample_body,
        grid_spec=pltpu.PrefetchScalarGridSpec(
            num_scalar_prefetch=1,
            grid=(batch,),
            in_specs=[pl.BlockSpec((t, QA_W), lambda b, s: (b, 0)),
                      pl.BlockSpec((t, 2 * KA_W), lambda b, s: (b, 0)), cache, cache],
            out_specs=[pl.BlockSpec((t, QA_W), lambda b, s: (b, 0)), cache, cache]),
        out_shape=[jax.ShapeDtypeStruct((m, QA_W), BF16),
                   jax.ShapeDtypeStruct((batch, WINDOW, KA_W), F32),
                   jax.ShapeDtypeStruct((batch, WINDOW, KA_W), F32)],
        compiler_params=_cparams(1),
        name="swa_sample",
    )(sink, qa, kva, cache_k, cache_v)


def _gla_nodes(t_len):
    nodes = []
    span = t_len
    while span > SUB_BLOCK:
        half = span // 2
        for start in range(0, t_len, span):
            nodes.append((start + half, start + span, start, start + half, start + half - 1))
        span = half
    return nodes


def _gla_chunks(chunks, st, tri_ref, ones_ref, stmask_ref):
    nt = (((1,), (1,)), ((), ()))
    tn = (((0,), (0,)), ((), ()))
    t_len = chunks[0][0].shape[0]
    row = lax.broadcasted_iota(jnp.int32, (t_len, QB_W), 0)
    lane_k = lax.broadcasted_iota(jnp.int32, (t_len, QB_W), 1) // DK_B
    slot = lax.broadcasted_iota(jnp.int32, (t_len, QB_W), 1) % DK_B
    lane_v = lax.broadcasted_iota(jnp.int32, (t_len, VB_W), 1) // DV_B
    tri = tri_ref[0:t_len, 0:t_len]
    nodes = _gla_nodes(t_len)

    bs = []
    for (q, k, g, v) in chunks:
        g_hi = g.astype(BF16)
        g_lo = (g - g_hi.astype(F32)).astype(BF16)
        bs.append(jnp.dot(tri, g_hi, preferred_element_type=F32) + jnp.dot(tri, g_lo, preferred_element_type=F32))

    outs, atts, coefs, v16s = [], [], [], []
    for (q, k, g, v), b in zip(chunks, bs):
        b_last = b[t_len - 1:t_len, :]
        v16 = v.astype(BF16)
        v16s.append(v16)

        outs.append(lax.dot_general((q * jnp.exp2(b)).astype(BF16), st.astype(BF16), nt,
                                    preferred_element_type=F32))
        kf = (k * jnp.exp2(b_last - b)).astype(BF16)
        upd = lax.dot_general(v16, kf, tn, preferred_element_type=F32)
        st = jnp.where(stmask_ref[...] > 0.0, jnp.exp2(b_last) * st + upd, 0.0)

        if nodes:
            qf_parts, kf_parts = [], []
            for (q_lo, q_hi, k_lo, k_hi, a) in nodes:
                anchor = b[a:a + 1, :]
                qf = jnp.where((row >= q_lo) & (row < q_hi), q * jnp.exp2(jnp.minimum(b - anchor, 0.0)), 0.0)
                kn = jnp.where((row >= k_lo) & (row < k_hi), k * jnp.exp2(jnp.minimum(anchor - b, 0.0)), 0.0)
                qf_parts.append(jnp.concatenate(
                    [jnp.where(lane_k == hb, qf, 0.0) for hb in range(H_B)], axis=0).astype(BF16))
                kf_parts.append(kn.astype(BF16))
            q_all = qf_parts[0] if len(nodes) == 1 else jnp.concatenate(qf_parts, axis=1)
            k_all = kf_parts[0] if len(nodes) == 1 else jnp.concatenate(kf_parts, axis=1)
            atts.append(lax.dot_general(q_all, k_all, nt, preferred_element_type=F32))

        e_parts = []
        for d in range(SUB_BLOCK):
            k_d = k if d == 0 else pltpu.roll(k, d, 0)
            b_d = b if d == 0 else pltpu.roll(b, d, 0)
            e_parts.append((q * k_d * jnp.exp2(jnp.minimum(b - b_d, 0.0))).astype(BF16))
        coefs.append(jnp.dot(jnp.concatenate(e_parts, axis=0), ones_ref[...],
                             preferred_element_type=F32))

    for i, v16 in enumerate(v16s):
        a_c = jnp.zeros((t_len, QB_W), F32)
        for d in range(SUB_BLOCK):
            pick = (slot == row - d) & ((row % SUB_BLOCK) >= d)
            a_c = a_c + jnp.where(pick, coefs[i][d * t_len:(d + 1) * t_len], 0.0)
        v_bd = []
        for hb in range(H_B):
            v_bd.append(jnp.where(lane_v == hb, v16, jnp.zeros_like(v16)))
            if t_len < DK_B:
                v_bd.append(jnp.zeros((DK_B - t_len, VB_W), BF16))
        v_bd = jnp.concatenate(v_bd, axis=0)
        o = outs[i] + jnp.dot(a_c.astype(BF16), v_bd, preferred_element_type=F32)
        if nodes:
            full = jnp.dot(atts[i].astype(BF16), v16, preferred_element_type=F32)
            for hb in range(H_B):
                o = o + jnp.where(lane_v == hb, full[hb * t_len:(hb + 1) * t_len], 0.0)
        outs[i] = o
    return outs, st


GLA_CHUNKS_PER_STEP = 4


def _gla_prompt_body(qkg_ref, v_ref, tri_ref, ones_ref, stmask_ref, o_ref, so_ref, st_ref):
    seq = qkg_ref.shape[0]

    def run(starts, t_len):
        rows = [pl.ds(r0, t_len) for r0 in starts]
        chunks = [(qkg_ref[r, 0:QB_W], qkg_ref[r, QB_W:2 * QB_W], qkg_ref[r, 2 * QB_W:3 * QB_W], v_ref[r, :])
                  for r in rows]
        outs, st = _gla_chunks(chunks, st_ref[...], tri_ref, ones_ref, stmask_ref)
        for r, o in zip(rows, outs):
            o_ref[r, :] = o
        st_ref[...] = st

    st_ref[...] = jnp.zeros(st_ref.shape, F32)
    run([0], N_META)

    def step(it, carry):
        base = N_META + it * (CHUNK * GLA_CHUNKS_PER_STEP)
        run([pl.multiple_of(base + u * CHUNK, 8) for u in range(GLA_CHUNKS_PER_STEP)], CHUNK)
        return carry

    lax.fori_loop(0, (seq - N_META) // (CHUNK * GLA_CHUNKS_PER_STEP), step, 0)
    so_ref[0] = st_ref[...]


def _gla_consts():
    i = np.arange(CHUNK)
    tri = (i[:, None] >= i[None, :]).astype(np.float32)
    kd = np.arange(QB_W) // DK_B
    vd = np.arange(VB_W) // DV_B
    ones = (kd[:, None] == kd[None, :]).astype(np.float32)
    stmask = (vd[:, None] == kd[None, :]).astype(np.float32)
    return jnp.asarray(tri, BF16), jnp.asarray(ones, BF16), jnp.asarray(stmask, F32)


def _gla_prompt(qkg, vb, *, batch):
    m = qkg.shape[0]
    seq = m // batch
    assert (seq - N_META) % (CHUNK * GLA_CHUNKS_PER_STEP) == 0
    tri, ones, stmask = _gla_consts()
    return pl.pallas_call(
        _gla_prompt_body,
        grid=(batch,),
        in_specs=[pl.BlockSpec((seq, 3 * QB_W), lambda b: (b, 0)),
                  pl.BlockSpec((seq, VB_W), lambda b: (b, 0)),
                  _const_spec(tri.shape), _const_spec(ones.shape), _const_spec(stmask.shape)],
        out_specs=[pl.BlockSpec((seq, VB_W), lambda b: (b, 0)),
                   pl.BlockSpec((1, VB_W, QB_W), lambda b: (b, 0, 0))],
        out_shape=[jax.ShapeDtypeStruct((m, VB_W), F32),
                   jax.ShapeDtypeStruct((batch, VB_W, QB_W), F32)],
        scratch_shapes=[pltpu.VMEM((VB_W, QB_W), F32)],
        compiler_params=_cparams(1),
        name="gla_prompt",
    )(qkg, vb, tri, ones, stmask)


def _gla_sample_body(qkg_ref, v_ref, si_ref, tri_ref, ones_ref, stmask_ref, o_ref, so_ref):
    outs, st = _gla_chunks([(qkg_ref[:, 0:QB_W], qkg_ref[:, QB_W:2 * QB_W], qkg_ref[:, 2 * QB_W:3 * QB_W],
                             v_ref[...])], si_ref[0], tri_ref, ones_ref, stmask_ref)
    o_ref[...] = outs[0]
    so_ref[0] = st


def _gla_sample(qkg, vb, state_t, *, batch):
    m = qkg.shape[0]
    t = m // batch
    tri, ones, stmask = _gla_consts()
    st_spec = pl.BlockSpec((1, VB_W, QB_W), lambda b: (b, 0, 0))
    return pl.pallas_call(
        _gla_sample_body,
        grid=(batch,),
        in_specs=[pl.BlockSpec((t, 3 * QB_W), lambda b: (b, 0)), pl.BlockSpec((t, VB_W), lambda b: (b, 0)),
                  st_spec, _const_spec(tri.shape), _const_spec(ones.shape), _const_spec(stmask.shape)],
        out_specs=[pl.BlockSpec((t, VB_W), lambda b: (b, 0)), st_spec],
        out_shape=[jax.ShapeDtypeStruct((m, VB_W), F32),
                   jax.ShapeDtypeStruct((batch, VB_W, QB_W), F32)],
        compiler_params=_cparams(1),
        name="gla_sample",
    )(qkg, vb, state_t, tri, ones, stmask)


def _state_to_blockdiag(s):
    bsz = s.shape[0]
    eye = jnp.eye(H_B, dtype=s.dtype)
    st = jnp.einsum('bhdv,hg->bhvgd', s, eye)
    return st.reshape(bsz, VB_W, QB_W)


def _state_from_blockdiag(st):
    bsz = st.shape[0]
    st5 = st.reshape(bsz, H_B, DV_B, H_B, DK_B)
    diag = jnp.stack([st5[:, hb, :, hb, :] for hb in range(H_B)], axis=1)
    return diag.transpose(0, 1, 3, 2)


FOX_TILE = MXU_DIM


def _cumsum_rows(x, tri_ref, carry):
    n = x.shape[0]
    tri = tri_ref[0:n, 0:n]
    hi = x.astype(BF16)
    r1 = x - hi.astype(F32)
    mid = r1.astype(BF16)
    lo = (r1 - mid.astype(F32)).astype(BF16)
    c = (jnp.dot(tri, hi, preferred_element_type=F32) + jnp.dot(tri, mid, preferred_element_type=F32)
         + jnp.dot(tri, lo, preferred_element_type=F32)) + carry
    return c, c[n - 1:n, :]


def _fox_aux(c):
    grp = (lax.broadcasted_iota(jnp.int32, c.shape, 1) % HEAD_DIM) // H_C
    c = c * LOG2E
    c1 = c.astype(BF16)
    r1 = c - c1.astype(F32)
    c2 = r1.astype(BF16)
    c3 = (r1 - c2.astype(F32)).astype(BF16)
    zero = jnp.zeros(c.shape, BF16)
    ak = jnp.where(grp == 0, -c1, jnp.where(grp == 1, -c2, jnp.where(grp == 2, -c3, zero)))
    aq = jnp.where(grp == 3, c1, zero)
    return ak, aq


def _fox_operand(x, aux, h, head_id, key_side):
    lane = lax.broadcasted_iota(jnp.int32, (1, x.shape[1]), 1)
    own = (lane // HEAD_DIM) == h
    sub = lane % HEAD_DIM
    if key_side:
        sel = sub == (3 * H_C + head_id)
    else:
        sel = ((sub % H_C) == head_id) & (sub < 3 * H_C)
    return jnp.where(own, x, jnp.where(sel, jnp.ones((), BF16), aux))


VT_ROWS = HEAD_DIM + SUB_BLOCK


def _fox_partial(s, vt, pad_keys=0):
    m = jnp.max(s, axis=0, keepdims=True)
    pb = jnp.exp2(s - m).astype(BF16)
    if pad_keys:
        pb = jnp.concatenate([pb, jnp.zeros((pad_keys, pb.shape[1]), BF16)], axis=0)
    return m, jnp.dot(vt, pb, preferred_element_type=F32)


def _fox_merge(parts):
    m = parts[0][0]
    for part in parts[1:]:
        m = jnp.maximum(m, part[0])
    o = None
    for (mj, oj) in parts:
        w = jnp.exp2(mj - m)
        o = w * oj if o is None else o + w * oj
    return m, o


def _vt_with_ones(vt_f32):
    n = vt_f32.shape[1]
    row = lax.broadcasted_iota(jnp.int32, (SUB_BLOCK, n), 0)
    tail = jnp.where(row == 0, 1.0, 0.0).astype(BF16)
    return jnp.concatenate([vt_f32.astype(BF16), tail], axis=0)


def _fox_scores(kaug, qaug):
    return lax.dot_general(kaug, qaug, (((1,), (1,)), ((), ())), preferred_element_type=F32)


def _causal(s):
    key = lax.broadcasted_iota(jnp.int32, s.shape, 0)
    qry = lax.broadcasted_iota(jnp.int32, s.shape, 1)
    return jnp.where(key <= qry, s, NEG)


def _fox_finish(state0, state1):
    outs = [st[1][0:HEAD_DIM] / st[1][HEAD_DIM:HEAD_DIM + 1] for st in (state0, state1)]
    return jnp.concatenate(outs, axis=0).T


def _fox_prompt_body(*refs, prev_layers):
    if prev_layers:
        (q_ref, k_ref, v_ref, lf_ref, tri_ref, pk_ref, pv_ref, o_ref, kto_ref, vto_ref,
         ak_ref, aq_ref, kaug_ref, vt_ref, vtm_ref) = refs
        for n in range(prev_layers):
            kto_ref[n, 0] = pk_ref[n, 0]
            vto_ref[n, 0] = pv_ref[n, 0]
    else:
        (q_ref, k_ref, v_ref, lf_ref, tri_ref, o_ref, kto_ref, vto_ref,
         ak_ref, aq_ref, kaug_ref, vt_ref, vtm_ref) = refs
    kt_out = kto_ref.at[prev_layers, 0]
    vt_out = vto_ref.at[prev_layers, 0]
    hp = pl.program_id(1)
    seq = q_ref.shape[0]
    n_tiles = (seq - N_META) // FOX_TILE

    @pl.when(hp == 0)
    def _():
        c, carry = _cumsum_rows(lf_ref[0:N_META, :], tri_ref, jnp.zeros((1, LANES), F32))
        ak, aq = _fox_aux(c)
        ak_ref[0:N_META, :] = ak
        aq_ref[0:N_META, :] = aq
        for j in range(n_tiles):
            r0 = N_META + j * FOX_TILE
            c, carry = _cumsum_rows(lf_ref[r0:r0 + FOX_TILE, :], tri_ref, carry)
            ak, aq = _fox_aux(c)
            ak_ref[r0:r0 + FOX_TILE, :] = ak
            aq_ref[r0:r0 + FOX_TILE, :] = aq

    heads = [2 * hp, 2 * hp + 1]
    kb = k_ref[...].astype(BF16)
    for h in range(2):
        kaug_ref[h] = _fox_operand(kb, ak_ref[...], h, heads[h], True)
    lane_m = lax.broadcasted_iota(jnp.int32, (LANES, LANES), 1)
    vt_head = v_ref[0:LANES, :].T
    vt_meta = jnp.where(lane_m < N_META, vt_head, 0.0)
    for h in range(2):
        vtm_ref[h] = _vt_with_ones(vt_meta[h * HEAD_DIM:(h + 1) * HEAD_DIM])
    vt_out[:, 0:N_META] = vt_head[:, 0:N_META]
    for j in range(n_tiles):
        r0 = N_META + j * FOX_TILE
        vt_tile = v_ref[r0:r0 + FOX_TILE, :].T
        for h in range(2):
            vt_ref[j, h] = _vt_with_ones(vt_tile[h * HEAD_DIM:(h + 1) * HEAD_DIM])
        vt_out[:, r0:r0 + FOX_TILE] = vt_tile
    body_rows = (seq // LANES) * LANES
    for r0 in range(0, body_rows, FOX_TILE):
        r1 = min(r0 + FOX_TILE, body_rows)
        kt_out[:, r0:r1] = k_ref[r0:r1, :].T
    if seq > body_rows:
        kt_out[:, body_rows:seq] = k_ref[seq - LANES:seq, :].T[:, LANES - (seq - body_rows):LANES]

    def q_operands(r0, tq):
        qt = q_ref[pl.ds(r0, tq), :]
        aqt = aq_ref[pl.ds(r0, tq), :]
        return [_fox_operand(qt, aqt, h, heads[h], False) for h in range(2)]

    def meta_scores(qaug, h):
        return _fox_scores(kaug_ref[h, 0:N_META, :], qaug[h])

    def tile_scores(qaug, h, j):
        k0 = N_META + j * FOX_TILE
        return _fox_scores(kaug_ref[h, k0:k0 + FOX_TILE, :], qaug[h])

    def meta_part(s, h):
        return _fox_partial(s, vtm_ref[h], LANES - N_META)

    qaug = q_operands(0, LANES)
    st = [meta_part(_causal(meta_scores(qaug, h)), h) for h in range(2)]
    o_ref[0:N_META, :] = _fox_finish(st[0], st[1])[0:N_META, :].astype(o_ref.dtype)

    for p in range(n_tiles // 2):
        pair = (p, n_tiles - 1 - p)
        qs = {i: q_operands(N_META + i * FOX_TILE, FOX_TILE) for i in pair}
        scores = {}
        for i in pair:
            for h in range(2):
                s0 = _fox_scores(kaug_ref[h, 0:N_META + FOX_TILE, :], qs[i][h])
                tiles = [s0[N_META:]] + [tile_scores(qs[i], h, j) for j in range(1, i + 1)]
                tiles[i] = _causal(tiles[i])
                scores[i, h] = [s0[0:N_META]] + tiles
        for i in pair:
            fin = []
            for h in range(2):
                parts = [meta_part(scores[i, h][0], h)]
                parts += [_fox_partial(scores[i, h][1 + j], vt_ref[j, h]) for j in range(i + 1)]
                fin.append(_fox_merge(parts))
            r0 = N_META + i * FOX_TILE
            o_ref[r0:r0 + FOX_TILE, :] = _fox_finish(fin[0], fin[1]).astype(o_ref.dtype)


def _fox_tri():
    i = np.arange(FOX_TILE)
    return jnp.asarray((i[:, None] >= i[None, :]).astype(np.float32), BF16)


def _fox_prompt(q, k, v, lf, prev_kt=None, prev_vt=None, *, batch):
    m = q.shape[0]
    seq = m // batch
    n_tiles = (seq - N_META) // FOX_TILE
    tri = _fox_tri()
    col = pl.BlockSpec((seq, LANES), lambda b, hp: (b, hp))
    assert n_tiles % 2 == 0, "query tiles are processed in pairs"
    prev_layers = 0 if prev_kt is None else prev_kt.shape[0]
    feat_t = pl.BlockSpec((prev_layers + 1, 1, LANES, seq), lambda b, hp: (0, b, hp, 0))
    feat_s = jax.ShapeDtypeStruct((prev_layers + 1, batch, C_W, seq), F32)
    prev_specs, prev_args = [], []
    if prev_layers:
        prev_specs = [pl.BlockSpec((prev_layers, 1, LANES, seq), lambda b, hp: (0, b, hp, 0))] * 2
        prev_args = [prev_kt, prev_vt]
    return pl.pallas_call(
        functools.partial(_fox_prompt_body, prev_layers=prev_layers),
        grid=(batch, H_C // 2),
        in_specs=[col, col, col, pl.BlockSpec((seq, LANES), lambda b, hp: (b, 0)), _const_spec(tri.shape)]
        + prev_specs,
        out_specs=[col, feat_t, feat_t],
        out_shape=[jax.ShapeDtypeStruct((m, C_W), BF16), feat_s, feat_s],
        scratch_shapes=[pltpu.VMEM((seq, LANES), BF16), pltpu.VMEM((seq, LANES), BF16),
                        pltpu.VMEM((2, seq, LANES), BF16),
                        pltpu.VMEM((n_tiles, 2, VT_ROWS, FOX_TILE), BF16),
                        pltpu.VMEM((2, VT_ROWS, LANES), BF16)],
        compiler_params=_cparams(2),
        name="fox_prompt",
    )(q, k, v, lf, tri, *prev_args)


def _fox_key_rows(kt, akt, h, head_id):
    row = lax.broadcasted_iota(jnp.int32, kt.shape, 0)
    own = (row // HEAD_DIM) == h
    sel = (row % HEAD_DIM) == (3 * H_C + head_id)
    return jnp.where(own, kt, jnp.where(sel, jnp.ones(kt.shape, BF16), akt))


def _fox_sample_body(q_ref, k_ref, v_ref, lf_ref, ckt_ref, cvt_ref, clf_ref, tri_ref, o_ref,
                     akt_ref, aq_ref):
    hp = pl.program_id(1)
    t = q_ref.shape[0]
    past = ckt_ref.shape[3]
    n_tiles = past // FOX_TILE
    nn = (((1,), (1,)), ((), ()))

    def pad_rows(x):
        return jnp.concatenate([x, jnp.zeros((LANES - t, LANES), x.dtype)], axis=0)

    @pl.when(hp == 0)
    def _():
        carry = jnp.zeros((1, LANES), F32)
        for j in range(n_tiles):
            r0 = j * FOX_TILE
            c, carry = _cumsum_rows(clf_ref[0, 0, r0:r0 + FOX_TILE, :], tri_ref, carry)
            akt_ref[:, r0:r0 + FOX_TILE] = _fox_aux(c)[0].astype(F32).T.astype(BF16)
        c, carry = _cumsum_rows(lf_ref[...], tri_ref, carry)
        ak, aq = _fox_aux(c)
        akt_ref[:, past:past + LANES] = pad_rows(ak.astype(F32)).T.astype(BF16)
        aq_ref[...] = aq

    heads = [2 * hp, 2 * hp + 1]
    kt_new = pad_rows(k_ref[...]).T.astype(BF16)
    vt_new = pad_rows(v_ref[...]).T.astype(BF16)
    kt_old = ckt_ref[0, 0].astype(BF16)
    vt_old = cvt_ref[0, 0].astype(BF16)
    qry = lax.broadcasted_iota(jnp.int32, (t, LANES), 0)
    key = lax.broadcasted_iota(jnp.int32, (t, LANES), 1)

    scores = []
    for h in range(2):
        qaug = _fox_operand(q_ref[...], aq_ref[...], h, heads[h], False)
        s_old = jnp.dot(qaug, _fox_key_rows(kt_old, akt_ref[:, 0:past], h, heads[h]),
                        preferred_element_type=F32)
        s_new = jnp.dot(qaug, _fox_key_rows(kt_new, akt_ref[:, past:past + LANES], h, heads[h]),
                        preferred_element_type=F32)
        scores.append((s_old, jnp.where(key <= qry, s_new, NEG)))
    outs = []
    for (s_old, s_new) in scores:
        m = jnp.maximum(jnp.max(s_old, axis=-1, keepdims=True), jnp.max(s_new, axis=-1, keepdims=True))
        p_old = jnp.exp2(s_old - m)
        p_new = jnp.exp2(s_new - m)
        den = jnp.sum(p_old, axis=-1, keepdims=True) + jnp.sum(p_new, axis=-1, keepdims=True)
        o = (lax.dot_general(p_old.astype(BF16), vt_old, nn, preferred_element_type=F32)
             + lax.dot_general(p_new.astype(BF16), vt_new, nn, preferred_element_type=F32))
        outs.append(o / den)
    o_ref[...] = jnp.where(key < HEAD_DIM, outs[0], outs[1]).astype(o_ref.dtype)


def _fox_sample(q, k, v, lf, cache_kt, cache_vt, cache_lf, *, batch, layer):
    m = q.shape[0]
    t = m // batch
    past = cache_kt.shape[3]
    tri = _fox_tri()
    col = pl.BlockSpec((t, LANES), lambda b, hp: (b, hp))
    ccol = pl.BlockSpec((1, 1, LANES, past), lambda b, hp: (layer, b, hp, 0))
    return pl.pallas_call(
        _fox_sample_body,
        grid=(batch, H_C // 2),
        in_specs=[col, col, col, pl.BlockSpec((t, LANES), lambda b, hp: (b, 0)), ccol, ccol,
                  pl.BlockSpec((1, 1, past, LANES), lambda b, hp: (layer, b, 0, 0)), _const_spec(tri.shape)],
        out_specs=col,
        out_shape=jax.ShapeDtypeStruct((m, C_W), BF16),
        scratch_shapes=[pltpu.VMEM((LANES, past + LANES), BF16), pltpu.VMEM((t, LANES), BF16)],
        compiler_params=_cparams(2),
        name="fox_sample",
    )(q, k, v, lf, cache_kt, cache_vt, cache_lf, tri)


def _head_block_diag():
    i = np.arange(MXU_DIM)
    same = (i[:, None] // HEAD_DIM) == (i[None, :] // HEAD_DIM)
    return jnp.asarray(same.astype(np.float32) / HEAD_DIM, dtype=BF16)


def _rope_tables(pos):
    half = HEAD_DIM // 2
    inv = ROPE_THETA ** (-jnp.arange(half, dtype=F32) / half)
    ang = pos.astype(F32)[:, None] * inv[None, :]
    cos = jnp.cos(ang)
    sin = jnp.sin(ang)
    cos64 = jnp.concatenate([cos, cos], axis=1)
    sin64 = jnp.concatenate([-sin, sin], axis=1)
    return jnp.tile(cos64, (1, LANES // HEAD_DIM)), jnp.tile(sin64, (1, LANES // HEAD_DIM))


def _prep_ab(w_in, qn, kn, w_gate, b_gate):
    w_main = jnp.pad(w_in, ((0, 0), (0, LANES - GATE_RANK))).astype(BF16)
    wgate = jnp.pad(w_gate, ((0, LANES - GATE_RANK), (0, 0))).astype(BF16)
    return (w_main, jnp.tile(qn, H_A)[None, :], jnp.tile(kn, KV_A)[None, :], wgate, b_gate[None, :])


def _prep_c(w_in, b_f, qn, kn):
    w_main = w_in[:, :3 * C_W].astype(BF16)
    rep = LANES // H_C
    w_f = jnp.tile(w_in[:, 3 * C_W:], (1, rep)).astype(BF16)
    return (w_main, w_f, jnp.tile(b_f, rep)[None, :],
            jnp.tile(qn, MXU_DIM // HEAD_DIM)[None, :], jnp.tile(kn, MXU_DIM // HEAD_DIM)[None, :])


def _rows_from_feature_major(xt):
    n, bsz, _, seq = xt.shape
    return xt.reshape(n, bsz, H_C, HEAD_DIM, seq).transpose(0, 1, 4, 2, 3)


PROMPT_TILES_PER_STREAM = 3
FRAME_TILE = 1024


def kernel(x_prompt, x_sample, cache_a_k, cache_a_v, state_b, cache_c_k, cache_c_v, cache_c_logf,
           meta_tokens, norm_mix, norm_mlp, w_in_ab, qnorm_a, knorm_a, sink_a, w_gate_b, b_gate_b,
           onorm_b, w_out_ab, w_in_c, b_f_c, qnorm_c, knorm_c, w_out_c, w_up, w_down):
    bp_, seq_in = x_prompt.shape[:2]
    bs_, t_new = x_sample.shape[:2]
    seq = N_META + seq_in
    past = cache_c_k.shape[2]
    depth = norm_mix.shape[0]
    tm_p = seq // PROMPT_TILES_PER_STREAM
    tm_s = bs_ * t_new

    meta = jnp.broadcast_to(meta_tokens.astype(x_prompt.dtype)[None], (bp_, N_META, D_MODEL))
    xp = jnp.concatenate([meta, x_prompt], axis=1).reshape(bp_ * seq, D_MODEL)
    xs = x_sample.reshape(bs_ * t_new, D_MODEL)

    bd = _head_block_diag()
    cos_p, sin_p = _rope_tables(jnp.arange(seq))
    cos_s, sin_s = _rope_tables(N_META + past + jnp.arange(t_new))
    cos_s, sin_s = jnp.tile(cos_s, (bs_, 1)), jnp.tile(sin_s, (bs_, 1))

    n_odd = cache_c_k.shape[0]
    cache_kt = cache_c_k.transpose(0, 1, 3, 4, 2).reshape(n_odd, bs_, C_W, past)
    cache_vt = cache_c_v.transpose(0, 1, 3, 4, 2).reshape(n_odd, bs_, C_W, past)
    cache_lf = jnp.tile(cache_c_logf, (1, 1, 1, LANES // H_C))

    kt_all, vt_all = None, None
    akp, avp, bpo, cfp = [], [], [], []
    aks, avs, bso, cks, cvs, cfs = [], [], [], [], [], []
    for l in range(depth):
        i = l // 2
        g_mix = norm_mix[l][None, :]
        g_mlp = norm_mlp[l][None, :]
        wup = w_up[l].astype(BF16)
        wdown = w_down[l].astype(BF16)
        if l % 2 == 0:
            w_main, qn, kn, wgate, bgate = _prep_ab(
                w_in_ab[i], qnorm_a[i], knorm_a[i], w_gate_b[i], b_gate_b[i])
            wout = w_out_ab[i].astype(BF16)
            onorm = onorm_b[i][None, :]

            qa, kva, qkg, vb, rb = _proj_ab(xp, g_mix, w_main, bd, qn, kn, cos_p, sin_p, wgate, bgate,
                                            tm=tm_p, n_pos_tiles=PROMPT_TILES_PER_STREAM)
            oa = _swa_prompt(sink_a[i], qa, kva, batch=bp_)
            ob, st = _gla_prompt(qkg, vb, batch=bp_)
            xp = _post_ab(xp, oa, ob, rb, onorm, wout, g_mlp, wup, wdown, tm=tm_p)
            kv3 = kva.reshape(bp_, seq, 2 * KA_W)[:, seq - WINDOW:, :]
            akp.append(kv3[:, :, 0:KA_W].reshape(bp_, WINDOW, KV_A, HEAD_DIM))
            avp.append(kv3[:, :, KA_W:].reshape(bp_, WINDOW, KV_A, HEAD_DIM))
            bpo.append(_state_from_blockdiag(st))

            qa, kva, qkg, vb, rb = _proj_ab(xs, g_mix, w_main, bd, qn, kn, cos_s, sin_s, wgate, bgate,
                                            tm=tm_s, n_pos_tiles=1)
            oa, nk, nv = _swa_sample(sink_a[i], qa, kva, cache_a_k[i].reshape(bs_, WINDOW, KA_W),
                                     cache_a_v[i].reshape(bs_, WINDOW, KA_W), batch=bs_)
            ob, st = _gla_sample(qkg, vb, _state_to_blockdiag(state_b[i]), batch=bs_)
            xs = _post_ab(xs, oa, ob, rb, onorm, wout, g_mlp, wup, wdown, tm=tm_s)
            aks.append(nk.reshape(bs_, WINDOW, KV_A, HEAD_DIM))
            avs.append(nv.reshape(bs_, WINDOW, KV_A, HEAD_DIM))
            bso.append(_state_from_blockdiag(st))
        else:
            w_main, w_f, b_f, qn, kn = _prep_c(w_in_c[i], b_f_c[i], qnorm_c[i], knorm_c[i])
            wout = w_out_c[i].astype(BF16)

            q, k, v, lf = _proj_c(xp, g_mix, w_main, w_f, b_f, bd, qn, kn, tm=tm_p)
            o, kt_all, vt_all = _fox_prompt(q, k, v, lf, kt_all, vt_all, batch=bp_)
            if l == depth - 1:
                y_frames = _post_c_frames(xp, o, wout, g_mlp, wup, wdown, batch=bp_, tm=FRAME_TILE)
            else:
                xp = _post_c(xp, o, wout, g_mlp, wup, wdown, tm=tm_p)
            cfp.append(lf[:, 0:H_C].reshape(bp_, seq, H_C))

            q, k, v, lf = _proj_c(xs, g_mix, w_main, w_f, b_f, bd, qn, kn, tm=tm_s)
            o = _fox_sample(q, k, v, lf, cache_kt, cache_vt, cache_lf, batch=bs_, layer=i)
            xs = _post_c(xs, o, wout, g_mlp, wup, wdown, tm=tm_s)
            cks.append(k.reshape(bs_, t_new, H_C, HEAD_DIM))
            cvs.append(v.reshape(bs_, t_new, H_C, HEAD_DIM))
            cfs.append(lf[:, 0:H_C].reshape(bs_, t_new, H_C))

    if depth % 2 == 0:
        y_prompt = y_frames.reshape(bp_, seq_in, D_MODEL)
    else:
        y_prompt = xp.reshape(bp_, seq, D_MODEL)[:, N_META:]
    y_sample = xs.reshape(bs_, t_new, D_MODEL)
    return (y_prompt, y_sample,
            jnp.stack(akp), jnp.stack(avp), jnp.stack(bpo),
            _rows_from_feature_major(kt_all), _rows_from_feature_major(vt_all), jnp.stack(cfp),
            jnp.stack(aks), jnp.stack(avs), jnp.stack(bso), jnp.stack(cks), jnp.stack(cvs), jnp.stack(cfs))
```

```python
import functools

import numpy as np
import jax
import jax.numpy as jnp
from jax import lax
from jax.experimental import pallas as pl
from jax.experimental.pallas import tpu as pltpu

F32 = jnp.float32
BF16 = jnp.bfloat16

D_MODEL = 1024
CHUNK = 64
N_META = 16
HEAD_DIM = 64
H_A = 8
KV_A = 2
G_A = H_A // KV_A
WINDOW = 128
H_B = 4
DK_B = 64
DV_B = 128
GATE_RANK = 16
GATE_TAU = 16.0
H_C = 16
D_FF = 4 * D_MODEL
ROPE_THETA = 10000.0
EPS = 1e-6
NEG = -1e30
SCALE = HEAD_DIM ** -0.5
LOG2E = 1.4426950408889634
QA_W = H_A * HEAD_DIM
KA_W = KV_A * HEAD_DIM
QB_W = H_B * DK_B
VB_W = H_B * DV_B
AB_MAIN = QA_W + 2 * KA_W + 2 * QB_W + 2 * VB_W
C_W = H_C * HEAD_DIM

LANES = 128
MXU_DIM = 256
VMEM_LIMIT = 56 * 1024 * 1024
SUB_BLOCK = 16


def _cparams(n_axes):
    return pltpu.CompilerParams(
        dimension_semantics=("arbitrary",) * n_axes, vmem_limit_bytes=VMEM_LIMIT)


def _const_spec(shape):
    nd = len(shape)
    return pl.BlockSpec(shape, lambda *_: (0,) * nd, pipeline_mode=pl.Buffered(1))


def _rmsnorm_rows(x, g):
    ms = jnp.mean(x * x, axis=-1, keepdims=True)
    return x * lax.rsqrt(ms + EPS) * g


def _log_sigmoid(x):
    return jnp.minimum(x, 0.0) - jnp.log1p(jnp.exp(-jnp.abs(x)))


def _head_mean_square(x, bd_ref):
    xx = (x * x).astype(BF16)
    w = x.shape[1]
    parts = []
    for s in range(0, w, MXU_DIM):
        e = min(s + MXU_DIM, w)
        parts.append(jnp.dot(xx[:, s:e], bd_ref[0:e - s, 0:e - s], preferred_element_type=F32))
    return parts[0] if len(parts) == 1 else jnp.concatenate(parts, axis=1)


def _rotate_half(x):
    w = x.shape[1]
    lane = lax.broadcasted_iota(jnp.int32, x.shape, 1)
    first = (lane % HEAD_DIM) < (HEAD_DIM // 2)
    return jnp.where(first, pltpu.roll(x, w - HEAD_DIM // 2, 1), pltpu.roll(x, HEAD_DIM // 2, 1))


def _proj_ab_body(x_ref, g_ref, w_ref, bd_ref, qn_ref, kn_ref, cos_ref, sin_ref,
                  wgate_ref, bgate_ref, qa_ref, kva_ref, qkg_ref, vb_ref, rb_ref):
    h = _rmsnorm_rows(x_ref[...], g_ref[...]).astype(BF16)

    z = jnp.dot(h, w_ref[:, 0:QA_W + 2 * KA_W], preferred_element_type=F32)
    cos = cos_ref[...]
    sin = sin_ref[...]

    qa = z[:, 0:QA_W]
    qa = qa * lax.rsqrt(_head_mean_square(qa, bd_ref) + EPS) * qn_ref[...]
    cos_q = jnp.concatenate([cos] * (QA_W // LANES), axis=1)
    sin_q = jnp.concatenate([sin] * (QA_W // LANES), axis=1)
    qa_ref[...] = (qa * cos_q + _rotate_half(qa) * sin_q).astype(qa_ref.dtype)

    ka = z[:, QA_W:QA_W + KA_W]
    ka = ka * lax.rsqrt(_head_mean_square(ka, bd_ref) + EPS) * kn_ref[...]
    kva_ref[:, 0:KA_W] = ka * cos + _rotate_half(ka) * sin
    kva_ref[:, KA_W:2 * KA_W] = z[:, QA_W + KA_W:QA_W + 2 * KA_W]

    c0 = QA_W + 2 * KA_W
    z = jnp.dot(h, w_ref[:, c0:c0 + 2 * QB_W], preferred_element_type=F32)
    qkg_ref[:, 0:QB_W] = z[:, 0:QB_W] * (DK_B ** -0.5)
    qkg_ref[:, QB_W:2 * QB_W] = z[:, QB_W:2 * QB_W]

    c1 = c0 + 2 * QB_W
    vb_ref[...] = jnp.dot(h, w_ref[:, c1:c1 + VB_W], preferred_element_type=F32)
    z = jnp.dot(h, w_ref[:, c1 + VB_W:c1 + 2 * VB_W + LANES], preferred_element_type=F32)
    rb_ref[...] = z[:, 0:VB_W]
    glr = z[:, VB_W:VB_W + LANES]
    gl = jnp.dot(glr.astype(BF16), wgate_ref[...], preferred_element_type=F32) + bgate_ref[...]
    qkg_ref[:, 2 * QB_W:3 * QB_W] = _log_sigmoid(gl) * (LOG2E / GATE_TAU)


def _proj_ab(x, g, w_main, bd, qn, kn, cos, sin, wgate, bgate, *, tm, n_pos_tiles):
    m = x.shape[0]
    row = lambda w: pl.BlockSpec((tm, w), lambda i: (i, 0))
    pos = pl.BlockSpec((tm, LANES), lambda i: (i % n_pos_tiles, 0))
    return pl.pallas_call(
        _proj_ab_body,
        grid=(m // tm,),
        in_specs=[row(D_MODEL), _const_spec((1, D_MODEL)), _const_spec(w_main.shape),
                  _const_spec(bd.shape), _const_spec(qn.shape),
                  _const_spec(kn.shape), pos, pos, _const_spec(wgate.shape), _const_spec(bgate.shape)],
        out_specs=[row(QA_W), row(2 * KA_W), row(3 * QB_W), row(VB_W), row(VB_W)],
        out_shape=[jax.ShapeDtypeStruct((m, QA_W), BF16), jax.ShapeDtypeStruct((m, 2 * KA_W), F32),
                   jax.ShapeDtypeStruct((m, 3 * QB_W), F32), jax.ShapeDtypeStruct((m, VB_W), F32),
                   jax.ShapeDtypeStruct((m, VB_W), F32)],
        compiler_params=_cparams(1),
        name="proj_ab",
    )(x, g, w_main, bd, qn, kn, cos, sin, wgate, bgate)


def _proj_c_body(x_ref, g_ref, w_ref, wf_ref, bf_ref, bd_ref, qn_ref, kn_ref,
                 q_ref, k_ref, v_ref, lf_ref):
    h = _rmsnorm_rows(x_ref[...], g_ref[...]).astype(BF16)
    z = jnp.dot(h, w_ref[:, 0:2 * C_W], preferred_element_type=F32)
    v_ref[...] = jnp.dot(h, w_ref[:, 2 * C_W:3 * C_W], preferred_element_type=F32)
    for s in range(0, C_W, MXU_DIM):
        zq = z[:, s:s + MXU_DIM]
        zq = zq * lax.rsqrt(_head_mean_square(zq, bd_ref) + EPS) * qn_ref[...]
        q_ref[:, s:s + MXU_DIM] = (zq * (SCALE * LOG2E)).astype(q_ref.dtype)
        zk = z[:, C_W + s:C_W + s + MXU_DIM]
        k_ref[:, s:s + MXU_DIM] = zk * lax.rsqrt(_head_mean_square(zk, bd_ref) + EPS) * kn_ref[...]
    fl = jnp.dot(h, wf_ref[...], preferred_element_type=F32) + bf_ref[...]
    lf_ref[...] = _log_sigmoid(fl)


def _proj_c(x, g, w_main, w_f, b_f, bd, qn, kn, *, tm):
    m = x.shape[0]
    row = lambda w: pl.BlockSpec((tm, w), lambda i: (i, 0))
    return pl.pallas_call(
        _proj_c_body,
        grid=(m // tm,),
        in_specs=[row(D_MODEL), _const_spec((1, D_MODEL)), _const_spec(w_main.shape),
                  _const_spec(w_f.shape), _const_spec(b_f.shape), _const_spec(bd.shape),
                  _const_spec(qn.shape), _const_spec(kn.shape)],
        out_specs=[row(C_W), row(C_W), row(C_W), row(LANES)],
        out_shape=[jax.ShapeDtypeStruct((m, C_W), BF16), jax.ShapeDtypeStruct((m, C_W), F32),
                   jax.ShapeDtypeStruct((m, C_W), F32), jax.ShapeDtypeStruct((m, LANES), F32)],
        compiler_params=_cparams(1),
        name="proj_c",
    )(x, g, w_main, w_f, b_f, bd, qn, kn)


def _mlp_tail(x1, g2_ref, wup_ref, wdown_ref, out_ref):
    h2 = _rmsnorm_rows(x1, g2_ref[...]).astype(BF16)
    acc = x1
    for c in range(0, D_FF, D_MODEL):
        u = jnp.dot(h2, wup_ref[:, c:c + D_MODEL], preferred_element_type=F32)
        u = jnp.maximum(u, 0.0)
        acc = acc + jnp.dot((u * u).astype(BF16), wdown_ref[c:c + D_MODEL, :], preferred_element_type=F32)
    out_ref[...] = acc


def _post_ab_body(x_ref, oa_ref, ob_ref, rb_ref, on_ref, wout_ref, g2_ref, wup_ref, wdown_ref, out_ref):
    ob = ob_ref[...]
    parts = []
    for hb in range(H_B):
        sl = ob[:, hb * DV_B:(hb + 1) * DV_B]
        parts.append(sl * lax.rsqrt(jnp.mean(sl * sl, axis=-1, keepdims=True) + EPS) * on_ref[...])
    rb = rb_ref[...]
    gate = rb / (1.0 + jnp.exp(-rb))
    obn = (jnp.concatenate(parts, axis=1) * gate).astype(BF16)
    y = jnp.dot(oa_ref[...], wout_ref[0:QA_W, :], preferred_element_type=F32)
    y = y + jnp.dot(obn, wout_ref[QA_W:QA_W + VB_W, :], preferred_element_type=F32)
    _mlp_tail(x_ref[...] + y, g2_ref, wup_ref, wdown_ref, out_ref)


def _post_c_body(x_ref, o_ref, wout_ref, g2_ref, wup_ref, wdown_ref, out_ref):
    y = jnp.dot(o_ref[...], wout_ref[...], preferred_element_type=F32)
    _mlp_tail(x_ref[...] + y, g2_ref, wup_ref, wdown_ref, out_ref)


def _post_ab(x, oa, ob, rb, onorm, wout, g2, wup, wdown, *, tm):
    m = x.shape[0]
    row = lambda w: pl.BlockSpec((tm, w), lambda i: (i, 0))
    return pl.pallas_call(
        _post_ab_body,
        grid=(m // tm,),
        in_specs=[row(D_MODEL), row(QA_W), row(VB_W), row(VB_W), _const_spec(onorm.shape),
                  _const_spec(wout.shape), _const_spec(g2.shape), _const_spec(wup.shape),
                  _const_spec(wdown.shape)],
        out_specs=row(D_MODEL),
        out_shape=jax.ShapeDtypeStruct((m, D_MODEL), F32),
        compiler_params=_cparams(1),
        name="post_ab",
    )(x, oa, ob, rb, onorm, wout, g2, wup, wdown)


def _post_c(x, o, wout, g2, wup, wdown, *, tm):
    m = x.shape[0]
    row = lambda w: pl.BlockSpec((tm, w), lambda i: (i, 0))
    return pl.pallas_call(
        _post_c_body,
        grid=(m // tm,),
        in_specs=[row(D_MODEL), row(C_W), _const_spec(wout.shape), _const_spec(g2.shape),
                  _const_spec(wup.shape), _const_spec(wdown.shape)],
        out_specs=row(D_MODEL),
        out_shape=jax.ShapeDtypeStruct((m, D_MODEL), F32),
        compiler_params=_cparams(1),
        name="post_c",
    )(x, o, wout, g2, wup, wdown)


def _post_c_frames(x, o, wout, g2, wup, wdown, *, batch, tm):
    seq = x.shape[0] // batch
    frames = seq - N_META
    per = frames // tm
    src = lambda w: pl.BlockSpec((pl.Element(tm), pl.Element(w)),
                                 lambda b, j: (pl.multiple_of(b * seq + N_META + j * tm, 8), 0))
    return pl.pallas_call(
        _post_c_body,
        grid=(batch, per),
        in_specs=[src(D_MODEL), src(C_W), _const_spec(wout.shape), _const_spec(g2.shape),
                  _const_spec(wup.shape), _const_spec(wdown.shape)],
        out_specs=pl.BlockSpec((tm, D_MODEL), lambda b, j: (b * per + j, 0)),
        out_shape=jax.ShapeDtypeStruct((batch * frames, D_MODEL), F32),
        compiler_params=_cparams(2),
        name="post_c_frames",
    )(x, o, wout, g2, wup, wdown)


SWA_FRONT = CHUNK - N_META + (WINDOW // CHUNK) * CHUNK
SWA_BAND = WINDOW + CHUNK
SWA_CHUNKS_PER_STEP = 8


def _dup_heads(x, hk):
    lane = lax.broadcasted_iota(jnp.int32, x.shape, 1)
    own = (lane < HEAD_DIM) if hk == 0 else (lane >= HEAD_DIM)
    return jnp.where(own, x, pltpu.roll(x, HEAD_DIM, 1))


def _swa_scores(qs, kband):
    tq = qs.shape[0]
    even = lax.broadcasted_iota(jnp.int32, (tq, LANES), 1) < HEAD_DIM
    zero = jnp.zeros((tq, LANES), qs.dtype)
    s0, s1 = qs[:, 0:LANES], qs[:, LANES:2 * LANES]
    qz = jnp.concatenate([jnp.where(even, s0, zero), jnp.where(even, zero, s0),
                          jnp.where(even, s1, zero), jnp.where(even, zero, s1)], axis=0)
    return lax.dot_general(qz, kband, (((1,), (1,)), ((), ())), preferred_element_type=F32) * SCALE


def _swa_attend(s, vband, valid, sinks):
    tq = s.shape[0] // G_A
    even = lax.broadcasted_iota(jnp.int32, (tq, LANES), 1) < HEAD_DIM
    if valid is not None:
        s = jnp.where(valid, s, NEG)
    rowh = lax.broadcasted_iota(jnp.int32, (G_A * tq, 1), 0) // tq
    sk = jnp.where(rowh == 0, sinks[0], jnp.where(rowh == 1, sinks[1],
                                                   jnp.where(rowh == 2, sinks[2], sinks[3])))
    m = jnp.maximum(jnp.max(s, axis=-1, keepdims=True), sk)
    p = jnp.exp(s - m)
    den = jnp.sum(p, axis=-1, keepdims=True) + jnp.exp(sk - m)
    o = jnp.dot(p.astype(BF16), vband, preferred_element_type=F32) / den
    return (jnp.where(even, o[0:tq], o[tq:2 * tq]),
            jnp.where(even, o[2 * tq:3 * tq], o[3 * tq:4 * tq]))


def _swa_prompt_body(sink_ref, q_ref, kv_ref, o_ref, kd_ref, vd_ref):
    seq = q_ref.shape[0]
    ka = kv_ref[:, 0:KA_W]
    va = kv_ref[:, KA_W:2 * KA_W]
    for hk in range(KV_A):
        kd_ref[hk, 0:SWA_FRONT, :] = jnp.zeros((SWA_FRONT, LANES), BF16)
        vd_ref[hk, 0:SWA_FRONT, :] = jnp.zeros((SWA_FRONT, LANES), BF16)
        kd_ref[hk, SWA_FRONT:SWA_FRONT + seq, :] = _dup_heads(ka, hk).astype(BF16)
        vd_ref[hk, SWA_FRONT:SWA_FRONT + seq, :] = _dup_heads(va, hk).astype(BF16)
    sinks = [[sink_ref[G_A * hk + g] for g in range(G_A)] for hk in range(KV_A)]

    def blocks(items, tq):
        work = [(r0, kb0, hk) for (r0, kb0) in items for hk in range(KV_A)]
        scores = [_swa_scores(q_ref[pl.ds(r0, tq), hk * G_A * HEAD_DIM:(hk + 1) * G_A * HEAD_DIM],
                              kd_ref[hk, pl.ds(kb0, SWA_BAND), :]) for (r0, kb0, hk) in work]
        col = lax.broadcasted_iota(jnp.int32, (G_A * tq, SWA_BAND), 1)
        for (r0, kb0, hk), s in zip(work, scores):
            o0, o1 = _swa_attend(s, vd_ref[hk, pl.ds(kb0, SWA_BAND), :], (col + kb0) >= SWA_FRONT, sinks[hk])
            c0 = hk * G_A * HEAD_DIM
            o_ref[pl.ds(r0, tq), c0:c0 + LANES] = o0.astype(o_ref.dtype)
            o_ref[pl.ds(r0, tq), c0 + LANES:c0 + 2 * LANES] = o1.astype(o_ref.dtype)

    blocks([(0, 0)], N_META)

    def step(it, carry):
        items = []
        for u in range(SWA_CHUNKS_PER_STEP):
            c = it * SWA_CHUNKS_PER_STEP + u
            items.append((pl.multiple_of(N_META + c * CHUNK, 16), pl.multiple_of((c + 1) * CHUNK, CHUNK)))
        blocks(items, CHUNK)
        return carry

    lax.fori_loop(0, (seq - N_META) // (CHUNK * SWA_CHUNKS_PER_STEP), step, 0)


def _swa_prompt(sink, qa, kva, *, batch):
    m = qa.shape[0]
    seq = m // batch
    assert (seq - N_META) % (CHUNK * SWA_CHUNKS_PER_STEP) == 0
    return pl.pallas_call(
        _swa_prompt_body,
        grid_spec=pltpu.PrefetchScalarGridSpec(
            num_scalar_prefetch=1,
            grid=(batch,),
            in_specs=[pl.BlockSpec((seq, QA_W), lambda b, s: (b, 0)),
                      pl.BlockSpec((seq, 2 * KA_W), lambda b, s: (b, 0))],
            out_specs=pl.BlockSpec((seq, QA_W), lambda b, s: (b, 0)),
            scratch_shapes=[pltpu.VMEM((KV_A, SWA_FRONT + seq, LANES), BF16),
                            pltpu.VMEM((KV_A, SWA_FRONT + seq, LANES), BF16)]),
        out_shape=jax.ShapeDtypeStruct((m, QA_W), BF16),
        compiler_params=_cparams(1),
        name="swa_prompt",
    )(sink, qa, kva)


def _swa_sample_body(sink_ref, q_ref, kv_ref, ck_ref, cv_ref, o_ref, nk_ref, nv_ref):
    t = q_ref.shape[0]
    kk = jnp.concatenate([ck_ref[0], kv_ref[:, 0:KA_W]], axis=0)
    vv = jnp.concatenate([cv_ref[0], kv_ref[:, KA_W:2 * KA_W]], axis=0)
    nk_ref[0] = kk[t:, :]
    nv_ref[0] = vv[t:, :]
    for hk in range(KV_A):
        sinks = [sink_ref[G_A * hk + g] for g in range(G_A)]
        c0 = hk * G_A * HEAD_DIM
        s = _swa_scores(q_ref[:, c0:c0 + G_A * HEAD_DIM], _dup_heads(kk, hk).astype(BF16))
        o0, o1 = _swa_attend(s, _dup_heads(vv, hk).astype(BF16), None, sinks)
        o_ref[:, c0:c0 + LANES] = o0.astype(o_ref.dtype)
        o_ref[:, c0 + LANES:c0 + 2 * LANES] = o1.astype(o_ref.dtype)


def _swa_sample(sink, qa, kva, cache_k, cache_v, *, batch):
    m = qa.shape[0]
    t = m // batch
    cache = pl.BlockSpec((1, WINDOW, KA_W), lambda b, s: (b, 0, 0))
    return pl.pallas_call(
        _swa_sample_body,
        grid_spec=pltpu.PrefetchScalarGridSpec(
            num_scalar_prefetch=1,
            grid=(batch,),
            in_specs=[pl.BlockSpec((t, QA_W), lambda b, s: (b, 0)),
                      pl.BlockSpec((t, 2 * KA_W), lambda b, s: (b, 0)), cache, cache],
            out_specs=[pl.BlockSpec((t, QA_W), lambda b, s: (b, 0)), cache, cache]),
        out_shape=[jax.ShapeDtypeStruct((m, QA_W), BF16),
                   jax.ShapeDtypeStruct((batch, WINDOW, KA_W), F32),
                   jax.ShapeDtypeStruct((batch, WINDOW, KA_W), F32)],
        compiler_params=_cparams(1),
        name="swa_sample",
    )(sink, qa, kva, cache_k, cache_v)


def _gla_nodes(t_len):
    nodes = []
    span = t_len
    while span > SUB_BLOCK:
        half = span // 2
        for start in range(0, t_len, span):
            nodes.append((start + half, start + span, start, start + half, start + half - 1))
        span = half
    return nodes


def _gla_chunks(chunks, st, tri_ref, ones_ref, stmask_ref):
    nt = (((1,), (1,)), ((), ()))
    tn = (((0,), (0,)), ((), ()))
    t_len = chunks[0][0].shape[0]
    row = lax.broadcasted_iota(jnp.int32, (t_len, QB_W), 0)
    lane_k = lax.broadcasted_iota(jnp.int32, (t_len, QB_W), 1) // DK_B
    slot = lax.broadcasted_iota(jnp.int32, (t_len, QB_W), 1) % DK_B
    lane_v = lax.broadcasted_iota(jnp.int32, (t_len, VB_W), 1) // DV_B
    tri = tri_ref[0:t_len, 0:t_len]
    nodes = _gla_nodes(t_len)

    bs = []
    for (q, k, g, v) in chunks:
        g_hi = g.astype(BF16)
        g_lo = (g - g_hi.astype(F32)).astype(BF16)
        bs.append(jnp.dot(tri, g_hi, preferred_element_type=F32) + jnp.dot(tri, g_lo, preferred_element_type=F32))

    outs, atts, coefs, v16s = [], [], [], []
    for (q, k, g, v), b in zip(chunks, bs):
        b_last = b[t_len - 1:t_len, :]
        v16 = v.astype(BF16)
        v16s.append(v16)

        outs.append(lax.dot_general((q * jnp.exp2(b)).astype(BF16), st.astype(BF16), nt,
                                    preferred_element_type=F32))
        kf = (k * jnp.exp2(b_last - b)).astype(BF16)
        upd = lax.dot_general(v16, kf, tn, preferred_element_type=F32)
        st = jnp.where(stmask_ref[...] > 0.0, jnp.exp2(b_last) * st + upd, 0.0)

        if nodes:
            qf_parts, kf_parts = [], []
            for (q_lo, q_hi, k_lo, k_hi, a) in nodes:
                anchor = b[a:a + 1, :]
                qf = jnp.where((row >= q_lo) & (row < q_hi), q * jnp.exp2(jnp.minimum(b - anchor, 0.0)), 0.0)
                kn = jnp.where((row >= k_lo) & (row < k_hi), k * jnp.exp2(jnp.minimum(anchor - b, 0.0)), 0.0)
                qf_parts.append(jnp.concatenate(
                    [jnp.where(lane_k == hb, qf, 0.0) for hb in range(H_B)], axis=0).astype(BF16))
                kf_parts.append(kn.astype(BF16))
            q_all = qf_parts[0] if len(nodes) == 1 else jnp.concatenate(qf_parts, axis=1)
            k_all = kf_parts[0] if len(nodes) == 1 else jnp.concatenate(kf_parts, axis=1)
            atts.append(lax.dot_general(q_all, k_all, nt, preferred_element_type=F32))

        e_parts = []
        for d in range(SUB_BLOCK):
            k_d = k if d == 0 else pltpu.roll(k, d, 0)
            b_d = b if d == 0 else pltpu.roll(b, d, 0)
            e_parts.append((q * k_d * jnp.exp2(jnp.minimum(b - b_d, 0.0))).astype(BF16))
        coefs.append(jnp.dot(jnp.concatenate(e_parts, axis=0), ones_ref[...],
                             preferred_element_type=F32))

    for i, v16 in enumerate(v16s):
        a_c = jnp.zeros((t_len, QB_W), F32)
        for d in range(SUB_BLOCK):
            pick = (slot == row - d) & ((row % SUB_BLOCK) >= d)
            a_c = a_c + jnp.where(pick, coefs[i][d * t_len:(d + 1) * t_len], 0.0)
        v_bd = []
        for hb in range(H_B):
            v_bd.append(jnp.where(lane_v == hb, v16, jnp.zeros_like(v16)))
            if t_len < DK_B:
                v_bd.append(jnp.zeros((DK_B - t_len, VB_W), BF16))
        v_bd = jnp.concatenate(v_bd, axis=0)
        o = outs[i] + jnp.dot(a_c.astype(BF16), v_bd, preferred_element_type=F32)
        if nodes:
            full = jnp.dot(atts[i].astype(BF16), v16, preferred_element_type=F32)
            for hb in range(H_B):
                o = o + jnp.where(lane_v == hb, full[hb * t_len:(hb + 1) * t_len], 0.0)
        outs[i] = o
    return outs, st


GLA_CHUNKS_PER_STEP = 8


def _gla_prompt_body(qkg_ref, v_ref, tri_ref, ones_ref, stmask_ref, o_ref, so_ref, st_ref):
    seq = qkg_ref.shape[0]

    def run(starts, t_len):
        rows = [pl.ds(r0, t_len) for r0 in starts]
        chunks = [(qkg_ref[r, 0:QB_W], qkg_ref[r, QB_W:2 * QB_W], qkg_ref[r, 2 * QB_W:3 * QB_W], v_ref[r, :])
                  for r in rows]
        outs, st = _gla_chunks(chunks, st_ref[...], tri_ref, ones_ref, stmask_ref)
        for r, o in zip(rows, outs):
            o_ref[r, :] = o
        st_ref[...] = st

    st_ref[...] = jnp.zeros(st_ref.shape, F32)
    run([0], N_META)

    def step(it, carry):
        base = N_META + it * (CHUNK * GLA_CHUNKS_PER_STEP)
        run([pl.multiple_of(base + u * CHUNK, 8) for u in range(GLA_CHUNKS_PER_STEP)], CHUNK)
        return carry

    lax.fori_loop(0, (seq - N_META) // (CHUNK * GLA_CHUNKS_PER_STEP), step, 0)
    so_ref[0] = st_ref[...]


def _gla_consts():
    i = np.arange(CHUNK)
    tri = (i[:, None] >= i[None, :]).astype(np.float32)
    kd = np.arange(QB_W) // DK_B
    vd = np.arange(VB_W) // DV_B
    ones = (kd[:, None] == kd[None, :]).astype(np.float32)
    stmask = (vd[:, None] == kd[None, :]).astype(np.float32)
    return jnp.asarray(tri, BF16), jnp.asarray(ones, BF16), jnp.asarray(stmask, F32)


def _gla_prompt(qkg, vb, *, batch):
    m = qkg.shape[0]
    seq = m // batch
    assert (seq - N_META) % (CHUNK * GLA_CHUNKS_PER_STEP) == 0
    tri, ones, stmask = _gla_consts()
    return pl.pallas_call(
        _gla_prompt_body,
        grid=(batch,),
        in_specs=[pl.BlockSpec((seq, 3 * QB_W), lambda b: (b, 0)),
                  pl.BlockSpec((seq, VB_W), lambda b: (b, 0)),
                  _const_spec(tri.shape), _const_spec(ones.shape), _const_spec(stmask.shape)],
        out_specs=[pl.BlockSpec((seq, VB_W), lambda b: (b, 0)),
                   pl.BlockSpec((1, VB_W, QB_W), lambda b: (b, 0, 0))],
        out_shape=[jax.ShapeDtypeStruct((m, VB_W), F32),
                   jax.ShapeDtypeStruct((batch, VB_W, QB_W), F32)],
        scratch_shapes=[pltpu.VMEM((VB_W, QB_W), F32)],
        compiler_params=_cparams(1),
        name="gla_prompt",
    )(qkg, vb, tri, ones, stmask)


def _gla_sample_body(qkg_ref, v_ref, si_ref, tri_ref, ones_ref, stmask_ref, o_ref, so_ref):
    outs, st = _gla_chunks([(qkg_ref[:, 0:QB_W], qkg_ref[:, QB_W:2 * QB_W], qkg_ref[:, 2 * QB_W:3 * QB_W],
                             v_ref[...])], si_ref[0], tri_ref, ones_ref, stmask_ref)
    o_ref[...] = outs[0]
    so_ref[0] = st


def _gla_sample(qkg, vb, state_t, *, batch):
    m = qkg.shape[0]
    t = m // batch
    tri, ones, stmask = _gla_consts()
    st_spec = pl.BlockSpec((1, VB_W, QB_W), lambda b: (b, 0, 0))
    return pl.pallas_call(
        _gla_sample_body,
        grid=(batch,),
        in_specs=[pl.BlockSpec((t, 3 * QB_W), lambda b: (b, 0)), pl.BlockSpec((t, VB_W), lambda b: (b, 0)),
                  st_spec, _const_spec(tri.shape), _const_spec(ones.shape), _const_spec(stmask.shape)],
        out_specs=[pl.BlockSpec((t, VB_W), lambda b: (b, 0)), st_spec],
        out_shape=[jax.ShapeDtypeStruct((m, VB_W), F32),
                   jax.ShapeDtypeStruct((batch, VB_W, QB_W), F32)],
        compiler_params=_cparams(1),
        name="gla_sample",
    )(qkg, vb, state_t, tri, ones, stmask)


def _state_to_blockdiag(s):
    bsz = s.shape[0]
    eye = jnp.eye(H_B, dtype=s.dtype)
    st = jnp.einsum('bhdv,hg->bhvgd', s, eye)
    return st.reshape(bsz, VB_W, QB_W)


def _state_from_blockdiag(st):
    bsz = st.shape[0]
    st5 = st.reshape(bsz, H_B, DV_B, H_B, DK_B)
    diag = jnp.stack([st5[:, hb, :, hb, :] for hb in range(H_B)], axis=1)
    return diag.transpose(0, 1, 3, 2)


FOX_TILE = MXU_DIM


def _cumsum_rows(x, tri_ref, carry):
    n = x.shape[0]
    tri = tri_ref[0:n, 0:n]
    hi = x.astype(BF16)
    r1 = x - hi.astype(F32)
    mid = r1.astype(BF16)
    lo = (r1 - mid.astype(F32)).astype(BF16)
    c = (jnp.dot(tri, hi, preferred_element_type=F32) + jnp.dot(tri, mid, preferred_element_type=F32)
         + jnp.dot(tri, lo, preferred_element_type=F32)) + carry
    return c, c[n - 1:n, :]


def _fox_aux(c):
    grp = (lax.broadcasted_iota(jnp.int32, c.shape, 1) % HEAD_DIM) // H_C
    c = c * LOG2E
    c1 = c.astype(BF16)
    r1 = c - c1.astype(F32)
    c2 = r1.astype(BF16)
    c3 = (r1 - c2.astype(F32)).astype(BF16)
    zero = jnp.zeros(c.shape, BF16)
    ak = jnp.where(grp == 0, -c1, jnp.where(grp == 1, -c2, jnp.where(grp == 2, -c3, zero)))
    aq = jnp.where(grp == 3, c1, zero)
    return ak, aq


def _fox_operand(x, aux, h, head_id, key_side):
    lane = lax.broadcasted_iota(jnp.int32, (1, x.shape[1]), 1)
    own = (lane // HEAD_DIM) == h
    sub = lane % HEAD_DIM
    if key_side:
        sel = sub == (3 * H_C + head_id)
    else:
        sel = ((sub % H_C) == head_id) & (sub < 3 * H_C)
    return jnp.where(own, x, jnp.where(sel, jnp.ones((), BF16), aux))


VT_ROWS = HEAD_DIM + SUB_BLOCK


def _fox_partial(s, vt, pad_keys=0):
    m = jnp.max(s, axis=0, keepdims=True)
    pb = jnp.exp2(s - m).astype(BF16)
    if pad_keys:
        pb = jnp.concatenate([pb, jnp.zeros((pad_keys, pb.shape[1]), BF16)], axis=0)
    return m, jnp.dot(vt, pb, preferred_element_type=F32)


def _fox_merge(parts):
    m = parts[0][0]
    for part in parts[1:]:
        m = jnp.maximum(m, part[0])
    o = None
    for (mj, oj) in parts:
        w = jnp.exp2(mj - m)
        o = w * oj if o is None else o + w * oj
    return m, o


def _vt_with_ones(vt_f32):
    n = vt_f32.shape[1]
    row = lax.broadcasted_iota(jnp.int32, (SUB_BLOCK, n), 0)
    tail = jnp.where(row == 0, 1.0, 0.0).astype(BF16)
    return jnp.concatenate([vt_f32.astype(BF16), tail], axis=0)


def _fox_scores(kaug, qaug):
    return lax.dot_general(kaug, qaug, (((1,), (1,)), ((), ())), preferred_element_type=F32)


def _causal(s):
    key = lax.broadcasted_iota(jnp.int32, s.shape, 0)
    qry = lax.broadcasted_iota(jnp.int32, s.shape, 1)
    return jnp.where(key <= qry, s, NEG)


def _fox_finish(state0, state1):
    outs = [st[1][0:HEAD_DIM] / st[1][HEAD_DIM:HEAD_DIM + 1] for st in (state0, state1)]
    return jnp.concatenate(outs, axis=0).T


def _fox_prompt_body(*refs, prev_layers):
    if prev_layers:
        (q_ref, k_ref, v_ref, lf_ref, tri_ref, pk_ref, pv_ref, o_ref, kto_ref, vto_ref,
         ak_ref, aq_ref, kaug_ref, vt_ref, vtm_ref) = refs
        for n in range(prev_layers):
            kto_ref[n, 0] = pk_ref[n, 0]
            vto_ref[n, 0] = pv_ref[n, 0]
    else:
        (q_ref, k_ref, v_ref, lf_ref, tri_ref, o_ref, kto_ref, vto_ref,
         ak_ref, aq_ref, kaug_ref, vt_ref, vtm_ref) = refs
    kt_out = kto_ref.at[prev_layers, 0]
    vt_out = vto_ref.at[prev_layers, 0]
    hp = pl.program_id(1)
    seq = q_ref.shape[0]
    n_tiles = (seq - N_META) // FOX_TILE

    @pl.when(hp == 0)
    def _():
        c, carry = _cumsum_rows(lf_ref[0:N_META, :], tri_ref, jnp.zeros((1, LANES), F32))
        ak, aq = _fox_aux(c)
        ak_ref[0:N_META, :] = ak
        aq_ref[0:N_META, :] = aq
        for j in range(n_tiles):
            r0 = N_META + j * FOX_TILE
            c, carry = _cumsum_rows(lf_ref[r0:r0 + FOX_TILE, :], tri_ref, carry)
            ak, aq = _fox_aux(c)
            ak_ref[r0:r0 + FOX_TILE, :] = ak
            aq_ref[r0:r0 + FOX_TILE, :] = aq

    heads = [2 * hp, 2 * hp + 1]
    kb = k_ref[...].astype(BF16)
    for h in range(2):
        kaug_ref[h] = _fox_operand(kb, ak_ref[...], h, heads[h], True)
    lane_m = lax.broadcasted_iota(jnp.int32, (LANES, LANES), 1)
    vt_head = v_ref[0:LANES, :].T
    vt_meta = jnp.where(lane_m < N_META, vt_head, 0.0)
    for h in range(2):
        vtm_ref[h] = _vt_with_ones(vt_meta[h * HEAD_DIM:(h + 1) * HEAD_DIM])
    vt_out[:, 0:N_META] = vt_head[:, 0:N_META]
    for j in range(n_tiles):
        r0 = N_META + j * FOX_TILE
        vt_tile = v_ref[r0:r0 + FOX_TILE, :].T
        for h in range(2):
            vt_ref[j, h] = _vt_with_ones(vt_tile[h * HEAD_DIM:(h + 1) * HEAD_DIM])
        vt_out[:, r0:r0 + FOX_TILE] = vt_tile
    body_rows = (seq // LANES) * LANES
    for r0 in range(0, body_rows, FOX_TILE):
        r1 = min(r0 + FOX_TILE, body_rows)
        kt_out[:, r0:r1] = k_ref[r0:r1, :].T
    if seq > body_rows:
        kt_out[:, body_rows:seq] = k_ref[seq - LANES:seq, :].T[:, LANES - (seq - body_rows):LANES]

    def q_operands(r0, tq):
        qt = q_ref[pl.ds(r0, tq), :]
        aqt = aq_ref[pl.ds(r0, tq), :]
        return [_fox_operand(qt, aqt, h, heads[h], False) for h in range(2)]

    def meta_scores(qaug, h):
        return _fox_scores(kaug_ref[h, 0:N_META, :], qaug[h])

    def tile_scores(qaug, h, j):
        k0 = N_META + j * FOX_TILE
        return _fox_scores(kaug_ref[h, k0:k0 + FOX_TILE, :], qaug[h])

    def meta_part(s, h):
        return _fox_partial(s, vtm_ref[h], LANES - N_META)

    qaug = q_operands(0, LANES)
    st = [meta_part(_causal(meta_scores(qaug, h)), h) for h in range(2)]
    o_ref[0:N_META, :] = _fox_finish(st[0], st[1])[0:N_META, :].astype(o_ref.dtype)

    for p in range(n_tiles // 2):
        pair = (p, n_tiles - 1 - p)
        qs = {i: q_operands(N_META + i * FOX_TILE, FOX_TILE) for i in pair}
        scores = {}
        for i in pair:
            for h in range(2):
                s0 = _fox_scores(kaug_ref[h, 0:N_META + FOX_TILE, :], qs[i][h])
                tiles = [s0[N_META:]] + [tile_scores(qs[i], h, j) for j in range(1, i + 1)]
                tiles[i] = _causal(tiles[i])
                scores[i, h] = [s0[0:N_META]] + tiles
        for i in pair:
            fin = []
            for h in range(2):
                parts = [meta_part(scores[i, h][0], h)]
                parts += [_fox_partial(scores[i, h][1 + j], vt_ref[j, h]) for j in range(i + 1)]
                fin.append(_fox_merge(parts))
            r0 = N_META + i * FOX_TILE
            o_ref[r0:r0 + FOX_TILE, :] = _fox_finish(fin[0], fin[1]).astype(o_ref.dtype)


def _fox_tri():
    i = np.arange(FOX_TILE)
    return jnp.asarray((i[:, None] >= i[None, :]).astype(np.float32), BF16)


def _fox_prompt(q, k, v, lf, prev_kt=None, prev_vt=None, *, batch):
    m = q.shape[0]
    seq = m // batch
    n_tiles = (seq - N_META) // FOX_TILE
    tri = _fox_tri()
    col = pl.BlockSpec((seq, LANES), lambda b, hp: (b, hp))
    assert n_tiles % 2 == 0, "query tiles are processed in pairs"
    prev_layers = 0 if prev_kt is None else prev_kt.shape[0]
    feat_t = pl.BlockSpec((prev_layers + 1, 1, LANES, seq), lambda b, hp: (0, b, hp, 0))
    feat_s = jax.ShapeDtypeStruct((prev_layers + 1, batch, C_W, seq), F32)
    prev_specs, prev_args = [], []
    if prev_layers:
        prev_specs = [pl.BlockSpec((prev_layers, 1, LANES, seq), lambda b, hp: (0, b, hp, 0))] * 2
        prev_args = [prev_kt, prev_vt]
    return pl.pallas_call(
        functools.partial(_fox_prompt_body, prev_layers=prev_layers),
        grid=(batch, H_C // 2),
        in_specs=[col, col, col, pl.BlockSpec((seq, LANES), lambda b, hp: (b, 0)), _const_spec(tri.shape)]
        + prev_specs,
        out_specs=[col, feat_t, feat_t],
        out_shape=[jax.ShapeDtypeStruct((m, C_W), BF16), feat_s, feat_s],
        scratch_shapes=[pltpu.VMEM((seq, LANES), BF16), pltpu.VMEM((seq, LANES), BF16),
                        pltpu.VMEM((2, seq, LANES), BF16),
                        pltpu.VMEM((n_tiles, 2, VT_ROWS, FOX_TILE), BF16),
                        pltpu.VMEM((2, VT_ROWS, LANES), BF16)],
        compiler_params=_cparams(2),
        name="fox_prompt",
    )(q, k, v, lf, tri, *prev_args)


def _fox_key_rows(kt, akt, h, head_id):
    row = lax.broadcasted_iota(jnp.int32, kt.shape, 0)
    own = (row // HEAD_DIM) == h
    sel = (row % HEAD_DIM) == (3 * H_C + head_id)
    return jnp.where(own, kt, jnp.where(sel, jnp.ones(kt.shape, BF16), akt))


def _fox_sample_body(q_ref, k_ref, v_ref, lf_ref, ckt_ref, cvt_ref, clf_ref, tri_ref, o_ref,
                     akt_ref, aq_ref):
    hp = pl.program_id(1)
    t = q_ref.shape[0]
    past = ckt_ref.shape[3]
    n_tiles = past // FOX_TILE
    nn = (((1,), (1,)), ((), ()))

    def pad_rows(x):
        return jnp.concatenate([x, jnp.zeros((LANES - t, LANES), x.dtype)], axis=0)

    @pl.when(hp == 0)
    def _():
        carry = jnp.zeros((1, LANES), F32)
        for j in range(n_tiles):
            r0 = j * FOX_TILE
            c, carry = _cumsum_rows(clf_ref[0, 0, r0:r0 + FOX_TILE, :], tri_ref, carry)
            akt_ref[:, r0:r0 + FOX_TILE] = _fox_aux(c)[0].astype(F32).T.astype(BF16)
        c, carry = _cumsum_rows(lf_ref[...], tri_ref, carry)
        ak, aq = _fox_aux(c)
        akt_ref[:, past:past + LANES] = pad_rows(ak.astype(F32)).T.astype(BF16)
        aq_ref[...] = aq

    heads = [2 * hp, 2 * hp + 1]
    kt_new = pad_rows(k_ref[...]).T.astype(BF16)
    vt_new = pad_rows(v_ref[...]).T.astype(BF16)
    kt_old = ckt_ref[0, 0].astype(BF16)
    vt_old = cvt_ref[0, 0].astype(BF16)
    qry = lax.broadcasted_iota(jnp.int32, (t, LANES), 0)
    key = lax.broadcasted_iota(jnp.int32, (t, LANES), 1)

    scores = []
    for h in range(2):
        qaug = _fox_operand(q_ref[...], aq_ref[...], h, heads[h], False)
        s_old = jnp.dot(qaug, _fox_key_rows(kt_old, akt_ref[:, 0:past], h, heads[h]),
                        preferred_element_type=F32)
        s_new = jnp.dot(qaug, _fox_key_rows(kt_new, akt_ref[:, past:past + LANES], h, heads[h]),
                        preferred_element_type=F32)
        scores.append((s_old, jnp.where(key <= qry, s_new, NEG)))
    outs = []
    for (s_old, s_new) in scores:
        m = jnp.maximum(jnp.max(s_old, axis=-1, keepdims=True), jnp.max(s_new, axis=-1, keepdims=True))
        p_old = jnp.exp2(s_old - m)
        p_new = jnp.exp2(s_new - m)
        den = jnp.sum(p_old, axis=-1, keepdims=True) + jnp.sum(p_new, axis=-1, keepdims=True)
        o = (lax.dot_general(p_old.astype(BF16), vt_old, nn, preferred_element_type=F32)
             + lax.dot_general(p_new.astype(BF16), vt_new, nn, preferred_element_type=F32))
        outs.append(o / den)
    o_ref[...] = jnp.where(key < HEAD_DIM, outs[0], outs[1]).astype(o_ref.dtype)


def _fox_sample(q, k, v, lf, cache_kt, cache_vt, cache_lf, *, batch, layer):
    m = q.shape[0]
    t = m // batch
    past = cache_kt.shape[3]
    tri = _fox_tri()
    col = pl.BlockSpec((t, LANES), lambda b, hp: (b, hp))
    ccol = pl.BlockSpec((1, 1, LANES, past), lambda b, hp: (layer, b, hp, 0))
    return pl.pallas_call(
        _fox_sample_body,
        grid=(batch, H_C // 2),
        in_specs=[col, col, col, pl.BlockSpec((t, LANES), lambda b, hp: (b, 0)), ccol, ccol,
                  pl.BlockSpec((1, 1, past, LANES), lambda b, hp: (layer, b, 0, 0)), _const_spec(tri.shape)],
        out_specs=col,
        out_shape=jax.ShapeDtypeStruct((m, C_W), BF16),
        scratch_shapes=[pltpu.VMEM((LANES, past + LANES), BF16), pltpu.VMEM((t, LANES), BF16)],
        compiler_params=_cparams(2),
        name="fox_sample",
    )(q, k, v, lf, cache_kt, cache_vt, cache_lf, tri)


def _head_block_diag():
    i = np.arange(MXU_DIM)
    same = (i[:, None] // HEAD_DIM) == (i[None, :] // HEAD_DIM)
    return jnp.asarray(same.astype(np.float32) / HEAD_DIM, dtype=BF16)


def _rope_tables(pos):
    half = HEAD_DIM // 2
    inv = ROPE_THETA ** (-jnp.arange(half, dtype=F32) / half)
    ang = pos.astype(F32)[:, None] * inv[None, :]
    cos = jnp.cos(ang)
    sin = jnp.sin(ang)
    cos64 = jnp.concatenate([cos, cos], axis=1)
    sin64 = jnp.concatenate([-sin, sin], axis=1)
    return jnp.tile(cos64, (1, LANES // HEAD_DIM)), jnp.tile(sin64, (1, LANES // HEAD_DIM))


def _prep_ab(w_in, qn, kn, w_gate, b_gate):
    w_main = jnp.pad(w_in, ((0, 0), (0, LANES - GATE_RANK))).astype(BF16)
    wgate = jnp.pad(w_gate, ((0, LANES - GATE_RANK), (0, 0))).astype(BF16)
    return (w_main, jnp.tile(qn, H_A)[None, :], jnp.tile(kn, KV_A)[None, :], wgate, b_gate[None, :])


def _prep_c(w_in, b_f, qn, kn):
    w_main = w_in[:, :3 * C_W].astype(BF16)
    rep = LANES // H_C
    w_f = jnp.tile(w_in[:, 3 * C_W:], (1, rep)).astype(BF16)
    return (w_main, w_f, jnp.tile(b_f, rep)[None, :],
            jnp.tile(qn, MXU_DIM // HEAD_DIM)[None, :], jnp.tile(kn, MXU_DIM // HEAD_DIM)[None, :])


def _rows_from_feature_major(xt):
    n, bsz, _, seq = xt.shape
    return xt.reshape(n, bsz, H_C, HEAD_DIM, seq).transpose(0, 1, 4, 2, 3)


PROMPT_TILES_PER_STREAM = 3
FRAME_TILE = 1024


def kernel(x_prompt, x_sample, cache_a_k, cache_a_v, state_b, cache_c_k, cache_c_v, cache_c_logf,
           meta_tokens, norm_mix, norm_mlp, w_in_ab, qnorm_a, knorm_a, sink_a, w_gate_b, b_gate_b,
           onorm_b, w_out_ab, w_in_c, b_f_c, qnorm_c, knorm_c, w_out_c, w_up, w_down):
    bp_, seq_in = x_prompt.shape[:2]
    bs_, t_new = x_sample.shape[:2]
    seq = N_META + seq_in
    past = cache_c_k.shape[2]
    depth = norm_mix.shape[0]
    tm_p = seq // PROMPT_TILES_PER_STREAM
    tm_s = bs_ * t_new

    meta = jnp.broadcast_to(meta_tokens.astype(x_prompt.dtype)[None], (bp_, N_META, D_MODEL))
    xp = jnp.concatenate([meta, x_prompt], axis=1).reshape(bp_ * seq, D_MODEL)
    xs = x_sample.reshape(bs_ * t_new, D_MODEL)

    bd = _head_block_diag()
    cos_p, sin_p = _rope_tables(jnp.arange(seq))
    cos_s, sin_s = _rope_tables(N_META + past + jnp.arange(t_new))
    cos_s, sin_s = jnp.tile(cos_s, (bs_, 1)), jnp.tile(sin_s, (bs_, 1))

    n_odd = cache_c_k.shape[0]
    cache_kt = cache_c_k.transpose(0, 1, 3, 4, 2).reshape(n_odd, bs_, C_W, past)
    cache_vt = cache_c_v.transpose(0, 1, 3, 4, 2).reshape(n_odd, bs_, C_W, past)
    cache_lf = jnp.tile(cache_c_logf, (1, 1, 1, LANES // H_C))

    kt_all, vt_all = None, None
    akp, avp, bpo, cfp = [], [], [], []
    aks, avs, bso, cks, cvs, cfs = [], [], [], [], [], []
    for l in range(depth):
        i = l // 2
        g_mix = norm_mix[l][None, :]
        g_mlp = norm_mlp[l][None, :]
        wup = w_up[l].astype(BF16)
        wdown = w_down[l].astype(BF16)
        if l % 2 == 0:
            w_main, qn, kn, wgate, bgate = _prep_ab(
                w_in_ab[i], qnorm_a[i], knorm_a[i], w_gate_b[i], b_gate_b[i])
            wout = w_out_ab[i].astype(BF16)
            onorm = onorm_b[i][None, :]

            qa, kva, qkg, vb, rb = _proj_ab(xp, g_mix, w_main, bd, qn, kn, cos_p, sin_p, wgate, bgate,
                                            tm=tm_p, n_pos_tiles=PROMPT_TILES_PER_STREAM)
            oa = _swa_prompt(sink_a[i], qa, kva, batch=bp_)
            ob, st = _gla_prompt(qkg, vb, batch=bp_)
            xp = _post_ab(xp, oa, ob, rb, onorm, wout, g_mlp, wup, wdown, tm=tm_p)
            kv3 = kva.reshape(bp_, seq, 2 * KA_W)[:, seq - WINDOW:, :]
            akp.append(kv3[:, :, 0:KA_W].reshape(bp_, WINDOW, KV_A, HEAD_DIM))
            avp.append(kv3[:, :, KA_W:].reshape(bp_, WINDOW, KV_A, HEAD_DIM))
            bpo.append(_state_from_blockdiag(st))

            qa, kva, qkg, vb, rb = _proj_ab(xs, g_mix, w_main, bd, qn, kn, cos_s, sin_s, wgate, bgate,
                                            tm=tm_s, n_pos_tiles=1)
            oa, nk, nv = _swa_sample(sink_a[i], qa, kva, cache_a_k[i].reshape(bs_, WINDOW, KA_W),
                                     cache_a_v[i].reshape(bs_, WINDOW, KA_W), batch=bs_)
            ob, st = _gla_sample(qkg, vb, _state_to_blockdiag(state_b[i]), batch=bs_)
            xs = _post_ab(xs, oa, ob, rb, onorm, wout, g_mlp, wup, wdown, tm=tm_s)
            aks.append(nk.reshape(bs_, WINDOW, KV_A, HEAD_DIM))
            avs.append(nv.reshape(bs_, WINDOW, KV_A, HEAD_DIM))
            bso.append(_state_from_blockdiag(st))
        else:
            w_main, w_f, b_f, qn, kn = _prep_c(w_in_c[i], b_f_c[i], qnorm_c[i], knorm_c[i])
            wout = w_out_c[i].astype(BF16)

            q, k, v, lf = _proj_c(xp, g_mix, w_main, w_f, b_f, bd, qn, kn, tm=tm_p)
            o, kt_all, vt_all = _fox_prompt(q, k, v, lf, kt_all, vt_all, batch=bp_)
            if l == depth - 1:
                y_frames = _post_c_frames(xp, o, wout, g_mlp, wup, wdown, batch=bp_, tm=FRAME_TILE)
            else:
                xp = _post_c(xp, o, wout, g_mlp, wup, wdown, tm=tm_p)
            cfp.append(lf[:, 0:H_C].reshape(bp_, seq, H_C))

            q, k, v, lf = _proj_c(xs, g_mix, w_main, w_f, b_f, bd, qn, kn, tm=tm_s)
            o = _fox_sample(q, k, v, lf, cache_kt, cache_vt, cache_lf, batch=bs_, layer=i)
            xs = _post_c(xs, o, wout, g_mlp, wup, wdown, tm=tm_s)
            cks.append(k.reshape(bs_, t_new, H_C, HEAD_DIM))
            cvs.append(v.reshape(bs_, t_new, H_C, HEAD_DIM))
            cfs.append(lf[:, 0:H_C].reshape(bs_, t_new, H_C))

    if depth % 2 == 0:
        y_prompt = y_frames.reshape(bp_, seq_in, D_MODEL)
    else:
        y_prompt = xp.reshape(bp_, seq, D_MODEL)[:, N_META:]
    y_sample = xs.reshape(bs_, t_new, D_MODEL)
    return (y_prompt, y_sample,
            jnp.stack(akp), jnp.stack(avp), jnp.stack(bpo),
            _rows_from_feature_major(kt_all), _rows_from_feature_major(vt_all), jnp.stack(cfp),
            jnp.stack(aks), jnp.stack(avs), jnp.stack(bso), jnp.stack(cks), jnp.stack(cvs), jnp.stack(cfs))
```
